```python
import jax, jax.numpy as jnp
from jax import lax
import numpy as np

D_MODEL = 2048
BATCH = 16
SEQ = 256
DEPTH = 2
DEC_BATCH = 2
DEC_SEQ = 4096
PAST_LEN = 512

GRID_W = 64
NORM_EPS = 1e-6
NEG_INF = -1e30
Q_BLOCK = 128
GLA_HEADS = 4
GLA_DK = 64
GLA_DV = 128
GLA_LR = 16
GLA_TAU = 16.0
GLA_CHUNK = 64
GLA_QK_W = GLA_HEADS * GLA_DK
GLA_V_W = GLA_HEADS * GLA_DV
DIFF_HEADS = 8
DIFF_DK = 64
DIFF_DV = 128
DIFF_QK_W = DIFF_HEADS * 2 * DIFF_DK
DIFF_V_W = DIFF_HEADS * DIFF_DV
ROPE_BASE = 10000.0
ROPE_NFREQ = DIFF_DK // 4
NA_HEADS = 8
NA_DK = 64
NA_W = NA_HEADS * NA_DK
NA_WIN_H = 8
NA_WIN_W = 16
NA_QCOLS = 16
NA_KCOLS = 32
NA_ROWS_PER_BLOCK = Q_BLOCK // GRID_W
D_FF = ((8 * D_MODEL // 3 + 255) // 256) * 256
IN_WIDTHS = (GLA_QK_W, GLA_QK_W, GLA_V_W, GLA_V_W, 2 * GLA_LR,
             DIFF_QK_W, DIFF_QK_W, DIFF_V_W,
             NA_W, NA_W, NA_W,
             3 * D_MODEL)
D_IN = sum(IN_WIDTHS)

kernel_name = "hybrid_dit_prefix_context_step"

F32 = jnp.float32


def rmsnorm(x, g):
    xf = x.astype(F32)
    y = xf * lax.rsqrt(jnp.mean(xf * xf, axis=-1, keepdims=True) + NORM_EPS)
    return (y * g.astype(F32)).astype(x.dtype)


def modulation(cond, w_ada, b_ada):
    m = jax.nn.silu(cond) @ w_ada + b_ada
    m = m[..., None, :]
    return jnp.split(m, 6, axis=-1)


def map_query_blocks(fn, q):
    B, T = q.shape[0], q.shape[1]
    n = T // Q_BLOCK
    qb = jnp.moveaxis(q.reshape(B, n, Q_BLOCK, *q.shape[2:]), 1, 0)
    out = jnp.moveaxis(lax.map(fn, qb), 0, 1)
    return out.reshape(B, T, *out.shape[3:])


def gla_scan(q, k, v, log_a, s0):
    B, T, H, _ = q.shape
    dv = v.shape[-1]
    C = GLA_CHUNK
    n = T // C
    rs = lambda a: a.astype(F32).reshape(B, n, C, H, a.shape[-1])
    qc, kc, vc, la = rs(q), rs(k), rs(v), rs(log_a)
    b = jnp.cumsum(la, axis=2)
    b_last = b[:, :, -1]
    b_mid = b[:, :, C // 2 - 1:C // 2]
    q_in = qc * jnp.exp(b - b_mid)
    k_in = kc * jnp.exp(b_mid - b)
    causal = jnp.tril(jnp.ones((C, C), bool))
    att = jnp.where(causal, jnp.einsum('bnihd,bnjhd->bnhij', q_in, k_in), 0.0)
    o_intra = jnp.einsum('bnhij,bnjhe->bnihe', att, vc)
    dS = jnp.einsum('bnjhd,bnjhe->bnhde', kc * jnp.exp(b_last[:, :, None] - b), vc)
    decay = jnp.exp(b_last)

    def step(S, inp):
        dec, ds = inp
        return dec[..., None] * S + ds, S

    S_fin, S_start = lax.scan(step, s0.astype(F32), (jnp.moveaxis(decay, 1, 0), jnp.moveaxis(dS, 1, 0)))
    S_start = jnp.moveaxis(S_start, 0, 1)
    o_inter = jnp.einsum('bnihd,bnhde->bnihe', qc * jnp.exp(b), S_start)
    o = (o_intra + o_inter).reshape(B, T, H, dv)
    return o.astype(v.dtype), S_fin


def gla_bidir(q, k, v, la_f, la_b, s0_f, s0_b):
    flip = lambda a: jnp.flip(a, axis=1)
    o_f, s_f = gla_scan(q, k, v, la_f, s0_f)
    o_b, s_b = gla_scan(flip(q), flip(k), flip(v), flip(la_b), s0_b)
    return o_f + flip(o_b), s_f, s_b


def axial_rope_tables(T):
    pos = jnp.arange(T)
    row = (pos // GRID_W).astype(F32)
    col = (pos % GRID_W).astype(F32)
    inv = ROPE_BASE ** (-jnp.arange(ROPE_NFREQ, dtype=F32) / ROPE_NFREQ)
    ang = jnp.stack([row[:, None] * inv, col[:, None] * inv], axis=1)
    return jnp.cos(ang), jnp.sin(ang)


def apply_axial_rope(x, cos, sin):
    T = x.shape[1]
    xs = x.astype(F32).reshape(*x.shape[:-1], 2, 2, ROPE_NFREQ)
    a, b = xs[..., 0, :], xs[..., 1, :]
    c = cos.reshape(T, 1, 1, 2, ROPE_NFREQ)
    s = sin.reshape(T, 1, 1, 2, ROPE_NFREQ)
    out = jnp.stack([a * c - b * s, b * c + a * s], axis=-2)
    return out.reshape(x.shape).astype(x.dtype)


def diff_attention(q, k, v, lam, lam_init, g_sub):
    scale = DIFF_DK ** -0.5

    def block(qb):
        s = jnp.einsum('bqhmd,bkhmd->bhmqk', qb, k).astype(F32) * scale
        p = jax.nn.softmax(s, axis=-1)
        a = p[:, :, 0] - lam * p[:, :, 1]
        return jnp.einsum('bhqk,bkhe->bqhe', a.astype(v.dtype), v)

    o = map_query_blocks(block, q)
    return rmsnorm(o, g_sub) * (1.0 - lam_init)


def context_attention(q, k, v):
    def block(qb):
        s = jnp.einsum('bqhd,bkhd->bhqk', qb, k).astype(F32)
        p = jax.nn.softmax(s, axis=-1).astype(v.dtype)
        return jnp.einsum('bhqk,bkhd->bqhd', p, v)
    return map_query_blocks(block, q)


def neighbourhood_attention(q, k, v, k_ctx, v_ctx, rpb):
    B, T, H, d = q.shape
    rows = T // GRID_W
    kh = min(NA_WIN_H, rows)
    ncb = GRID_W // NA_QCOLS
    R = NA_ROWS_PER_BLOCK
    n_blocks = rows // R
    qg = q.reshape(B, rows, GRID_W, H, d)
    kg = k.reshape(B, rows, GRID_W, H, d)
    vg = v.reshape(B, rows, GRID_W, H, d)
    qcol = np.arange(GRID_W).reshape(ncb, NA_QCOLS)
    col_start = np.clip(qcol - NA_WIN_W // 2, 0, GRID_W - NA_WIN_W)
    kcol = np.clip(np.arange(ncb) * NA_QCOLS - NA_WIN_W // 2, 0, GRID_W - NA_KCOLS)[:, None] + np.arange(NA_KCOLS)
    col_ok = (kcol[:, None, :] >= col_start[:, :, None]) & (kcol[:, None, :] < col_start[:, :, None] + NA_WIN_W)
    mask = np.broadcast_to(col_ok[:, :, None, :], (ncb, NA_QCOLS, kh, NA_KCOLS)).reshape(ncb, NA_QCOLS, kh * NA_KCOLS)
    dcol = np.clip(kcol[:, None, :] - qcol[:, :, None], 1 - NA_WIN_W, NA_WIN_W - 1) + NA_WIN_W - 1
    idx_c = kcol[None, :, None, :]
    n_loc = kh * NA_KCOLS

    def block(rb):
        qrow = rb * R + jnp.arange(R)
        krow = jnp.clip(qrow - NA_WIN_H // 2, 0, rows - kh)[:, None] + jnp.arange(kh)
        q_blk = lax.dynamic_slice_in_dim(qg, rb * R, R, axis=1).reshape(B, R, ncb, NA_QCOLS, H, d)
        idx_r = krow[:, None, :, None]
        k_blk = kg[:, idx_r, idx_c].reshape(B, R, ncb, n_loc, H, d)
        v_blk = vg[:, idx_r, idx_c].reshape(B, R, ncb, n_loc, H, d)
        drow = krow - qrow[:, None] + NA_WIN_H - 1
        bias = rpb[:, drow[:, None, None, :, None], dcol[None, :, :, None, :]]
        bias = bias.reshape(H, R, ncb, NA_QCOLS, n_loc).astype(F32)
        s_loc = jnp.einsum('brcqhd,brckhd->bhrcqk', q_blk, k_blk).astype(F32) + bias
        s_loc = jnp.where(mask, s_loc, NEG_INF)
        s_ctx = jnp.einsum('brcqhd,bkhd->bhrcqk', q_blk, k_ctx).astype(F32)
        p = jax.nn.softmax(jnp.concatenate([s_loc, s_ctx], axis=-1), axis=-1).astype(v.dtype)
        o = (jnp.einsum('bhrcqk,brckhd->brcqhd', p[..., :n_loc], v_blk)
             + jnp.einsum('bhrcqk,bkhd->brcqhd', p[..., n_loc:], v_ctx))
        return o.reshape(B, R * GRID_W, H, d)

    out = lax.map(block, jnp.arange(n_blocks))
    return jnp.moveaxis(out, 0, 1).reshape(B, T, H, d)


def trunk_layer(x, cond, lam_init, w_ada, b_ada, norm_mix_g, w_in, w_decay, b_decay, gla_norm_g,
                diff_lambda, diff_norm_g, na_rpb, w_br_a, w_br_b, w_br_c, w_out,
                norm_ffn_g, w_ffn_gate, w_ffn_up, w_ffn_down, cache):
    B, T, _ = x.shape
    sh_a, sc_a, gt_a, sh_f, sc_f, gt_f = modulation(cond, w_ada, b_ada)
    h = rmsnorm(x, norm_mix_g) * (1.0 + sc_a) + sh_a
    offs, acc = [], 0
    for w in IN_WIDTHS[:-1]:
        acc += w
        offs.append(acc)
    (gq, gk, gv, gr, glr, dq, dk, dv, nq, nk, nv, gates) = jnp.split(h @ w_in, offs, axis=-1)

    gq = gq.reshape(B, T, GLA_HEADS, GLA_DK) * GLA_DK ** -0.5
    gk = gk.reshape(B, T, GLA_HEADS, GLA_DK)
    gv = gv.reshape(B, T, GLA_HEADS, GLA_DV)
    pre = jnp.einsum('btmr,mrk->btmk', glr.reshape(B, T, 2, GLA_LR), w_decay) + b_decay
    la = (jax.nn.log_sigmoid(pre.astype(F32)) / GLA_TAU).reshape(B, T, 2, GLA_HEADS, GLA_DK)
    if cache is None:
        s0_f = jnp.zeros((B, GLA_HEADS, GLA_DK, GLA_DV), F32)
        s0_b = s0_f
    else:
        s0_f, s0_b = cache[4], cache[5]
    o_a, s_f, s_b = gla_bidir(gq, gk, gv, la[:, :, 0], la[:, :, 1], s0_f, s0_b)
    o_a = rmsnorm(o_a, gla_norm_g).reshape(B, T, GLA_V_W) * jax.nn.silu(gr)
    y_a = o_a @ w_br_a

    dq = dq.reshape(B, T, DIFF_HEADS, 2, DIFF_DK)
    dk = dk.reshape(B, T, DIFF_HEADS, 2, DIFF_DK)
    dv = dv.reshape(B, T, DIFF_HEADS, DIFF_DV)
    lp = diff_lambda.astype(F32)
    lam = jnp.exp(jnp.sum(lp[0] * lp[1])) - jnp.exp(jnp.sum(lp[2] * lp[3])) + lam_init
    if cache is None:
        keys_b, vals_b, q_b = dk, dv, dq
    else:
        cos, sin = axial_rope_tables(T)
        q_b = apply_axial_rope(dq, cos, sin)
        S = cache[0].shape[1]
        keys_b = jnp.concatenate([cache[0].reshape(B, S, DIFF_HEADS, 2, DIFF_DK), apply_axial_rope(dk, cos, sin)], axis=1)
        vals_b = jnp.concatenate([cache[1], dv], axis=1)
    o_b = diff_attention(q_b, keys_b, vals_b, lam, lam_init, diff_norm_g)
    y_b = o_b.reshape(B, T, DIFF_V_W) @ w_br_b

    nq = nq.reshape(B, T, NA_HEADS, NA_DK) * NA_DK ** -0.5
    nk = nk.reshape(B, T, NA_HEADS, NA_DK)
    nv = nv.reshape(B, T, NA_HEADS, NA_DK)
    if cache is None:
        o_c = context_attention(nq, nk, nv)
    else:
        o_c = neighbourhood_attention(nq, nk, nv, cache[2], cache[3], na_rpb)
    y_c = o_c.reshape(B, T, NA_W) @ w_br_c

    g_a, g_b, g_c = jnp.split(jax.nn.sigmoid(gates), 3, axis=-1)
    x = x + gt_a * ((g_a * y_a + g_b * y_b + g_c * y_c) @ w_out)

    h2 = rmsnorm(x, norm_ffn_g) * (1.0 + sc_f) + sh_f
    x = x + gt_f * ((jax.nn.silu(h2 @ w_ffn_gate) * (h2 @ w_ffn_up)) @ w_ffn_down)

    ctx_state = None
    if cache is None:
        ctx_state = (dk.reshape(B, T, DIFF_HEADS, 2 * DIFF_DK), dv, nk, nv,
                     s_f.astype(x.dtype), s_b.astype(x.dtype))
    return x, ctx_state


def setup_inputs(seed: int = 0) -> dict:
    key = jax.random.key(seed)
    ks = jax.random.split(key, 32)
    nrm = lambda k, shape, s: jax.random.normal(k, shape, F32) * s
    D = D_MODEL
    return {
        "x_prompt": nrm(ks[0], (BATCH, SEQ, D), 1.0),
        "x_sample": nrm(ks[1], (DEC_BATCH, DEC_SEQ, D), 1.0),
        "c": nrm(ks[2], (DEC_BATCH, D), 1.0),
        "cache_diff_k": nrm(ks[3], (DEC_BATCH, DEPTH, PAST_LEN, DIFF_HEADS, 2 * DIFF_DK), 1.0),
        "cache_diff_v": nrm(ks[4], (DEC_BATCH, DEPTH, PAST_LEN, DIFF_HEADS, DIFF_DV), 1.0),
        "cache_na_k": nrm(ks[5], (DEC_BATCH, DEPTH, PAST_LEN, NA_HEADS, NA_DK), 1.0),
        "cache_na_v": nrm(ks[6], (DEC_BATCH, DEPTH, PAST_LEN, NA_HEADS, NA_DK), 1.0),
        "state_gla_fwd": nrm(ks[7], (DEC_BATCH, DEPTH, GLA_HEADS, GLA_DK, GLA_DV), 0.5),
        "state_gla_bwd": nrm(ks[8], (DEC_BATCH, DEPTH, GLA_HEADS, GLA_DK, GLA_DV), 0.5),
        "c_ctx": nrm(ks[9], (D,), 1.0),
        "w_ada": nrm(ks[10], (DEPTH, D, 6 * D), 0.5 * D ** -0.5),
        "b_ada": nrm(ks[11], (DEPTH, 6 * D), 0.01),
        "norm_mix_g": 1.0 + nrm(ks[12], (DEPTH, D), 0.01),
        "w_in": nrm(ks[13], (DEPTH, D, D_IN), D ** -0.5),
        "w_decay": nrm(ks[14], (DEPTH, 2, GLA_LR, GLA_QK_W), GLA_LR ** -0.5),
        "b_decay": nrm(ks[15], (DEPTH, 2, GLA_QK_W), 0.1),
        "gla_norm_g": 1.0 + nrm(ks[16], (DEPTH, GLA_DV), 0.01),
        "diff_lambda": nrm(ks[17], (DEPTH, 4, DIFF_DK), 0.1),
        "diff_norm_g": 1.0 + nrm(ks[18], (DEPTH, DIFF_DV), 0.01),
        "na_rpb": nrm(ks[19], (DEPTH, NA_HEADS, 2 * NA_WIN_H - 1, 2 * NA_WIN_W - 1), 0.1),
        "w_br_a": nrm(ks[20], (DEPTH, GLA_V_W, D), GLA_V_W ** -0.5),
        "w_br_b": nrm(ks[21], (DEPTH, DIFF_V_W, D), DIFF_V_W ** -0.5),
        "w_br_c": nrm(ks[22], (DEPTH, NA_W, D), NA_W ** -0.5),
        "w_out": nrm(ks[23], (DEPTH, D, D), D ** -0.5),
        "norm_ffn_g": 1.0 + nrm(ks[24], (DEPTH, D), 0.01),
        "w_ffn_gate": nrm(ks[25], (DEPTH, D, D_FF), D ** -0.5),
        "w_ffn_up": nrm(ks[26], (DEPTH, D, D_FF), D ** -0.5),
        "w_ffn_down": nrm(ks[27], (DEPTH, D_FF, D), D_FF ** -0.5),
        "norm_final_g": 1.0 + nrm(ks[28], (D,), 0.01),
    }


def reference(x_prompt, x_sample, c, cache_diff_k, cache_diff_v, cache_na_k, cache_na_v,
              state_gla_fwd, state_gla_bwd, c_ctx, w_ada, b_ada, norm_mix_g, w_in, w_decay, b_decay,
              gla_norm_g, diff_lambda, diff_norm_g, na_rpb, w_br_a, w_br_b, w_br_c, w_out,
              norm_ffn_g, w_ffn_gate, w_ffn_up, w_ffn_down, norm_final_g):
    def run(x, cond, l, cache):
        lam_init = 0.8 - 0.6 * float(np.exp(-0.3 * l))
        return trunk_layer(x, cond, lam_init, w_ada[l], b_ada[l], norm_mix_g[l], w_in[l], w_decay[l],
                           b_decay[l], gla_norm_g[l], diff_lambda[l], diff_norm_g[l], na_rpb[l],
                           w_br_a[l], w_br_b[l], w_br_c[l], w_out[l], norm_ffn_g[l],
                           w_ffn_gate[l], w_ffn_up[l], w_ffn_down[l], cache)

    xp = x_prompt
    states = []
    for l in range(DEPTH):
        xp, st = run(xp, c_ctx, l, None)
        states.append(st)
    y_prompt = rmsnorm(xp, norm_final_g)
    new_diff_k = jnp.stack([s[0] for s in states], axis=1)
    new_diff_v = jnp.stack([s[1] for s in states], axis=1)
    new_na_k = jnp.stack([s[2] for s in states], axis=1)
    new_na_v = jnp.stack([s[3] for s in states], axis=1)
    new_gla_fwd = jnp.stack([s[4] for s in states], axis=1)
    new_gla_bwd = jnp.stack([s[5] for s in states], axis=1)

    xs = x_sample
    for l in range(DEPTH):
        cache_l = (cache_diff_k[:, l], cache_diff_v[:, l], cache_na_k[:, l], cache_na_v[:, l],
                   state_gla_fwd[:, l], state_gla_bwd[:, l])
        xs, _ = run(xs, c, l, cache_l)
    y_sample = rmsnorm(xs, norm_final_g)

    return (y_prompt, y_sample, new_diff_k, new_diff_v, new_na_k, new_na_v, new_gla_fwd, new_gla_bwd)
```

```python
import functools

import numpy as np
import jax
import jax.numpy as jnp
from jax import lax
from jax.experimental import pallas as pl
from jax.experimental.pallas import tpu as pltpu

F32 = jnp.float32
BF16 = jnp.bfloat16

D_MODEL = 2048
BATCH = 16
SEQ = 256
DEPTH = 2
DEC_BATCH = 2
DEC_SEQ = 4096
PAST_LEN = 512
GRID_W = 64
GRID_H = DEC_SEQ // GRID_W
NORM_EPS = 1e-6
NEG_INF = -1e30
GLA_HEADS = 4
GLA_DK = 64
GLA_DV = 128
GLA_LR = 16
GLA_TAU = 16.0
GLA_CHUNK = 64
GLA_QK_W = GLA_HEADS * GLA_DK
GLA_V_W = GLA_HEADS * GLA_DV
DIFF_HEADS = 8
DIFF_DK = 64
DIFF_DV = 128
DIFF_QK_W = DIFF_HEADS * 2 * DIFF_DK
DIFF_V_W = DIFF_HEADS * DIFF_DV
ROPE_BASE = 10000.0
ROPE_NFREQ = DIFF_DK // 4
NA_HEADS = 8
NA_DK = 64
NA_W = NA_HEADS * NA_DK
NA_WIN_H = 8
NA_WIN_W = 16
D_FF = ((8 * D_MODEL // 3 + 255) // 256) * 256
IN_WIDTHS = (GLA_QK_W, GLA_QK_W, GLA_V_W, GLA_V_W, 2 * GLA_LR,
             DIFF_QK_W, DIFF_QK_W, DIFF_V_W, NA_W, NA_W, NA_W, 3 * D_MODEL)

M_CTX = BATCH * SEQ
M_DEC = DEC_BATCH * DEC_SEQ
M_ALL = M_CTX + M_DEC
N_COND = 1 + DEC_BATCH

LANES = 128
VMEM_LIMIT_BYTES = 56 * 1024 * 1024

GLA_A_W = 13 * LANES
GLA_BLOCK = 256
NA_QROWS = 2
NA_QBLK = NA_QROWS * GRID_W
NA_KROWS = NA_WIN_H + NA_QROWS - 1
NA_NLOC = NA_KROWS * GRID_W


def _params(sem):
    return pltpu.CompilerParams(dimension_semantics=sem, vmem_limit_bytes=VMEM_LIMIT_BYTES)


def _cond_of_row(row):
    return jnp.where(row < M_CTX, 0, 1 + (row - M_CTX) // DEC_SEQ)


def _dot(a, b):
    return jnp.dot(a, b, preferred_element_type=F32)


def _dot_nt(a, b):
    return lax.dot_general(a, b, (((1,), (1,)), ((), ())), preferred_element_type=F32)


def _dot_tn(a, b):
    return lax.dot_general(a, b, (((0,), (0,)), ((), ())), preferred_element_type=F32)


def _split3(x):
    hi = x.astype(BF16)
    r1 = x - hi.astype(F32)
    mid = r1.astype(BF16)
    lo = (r1 - mid.astype(F32)).astype(BF16)
    return hi, mid, lo


def _mod_kernel(c_ref, w_ref, b_ref, o_ref):
    c = c_ref[...]
    a = (c * jax.nn.sigmoid(c)).astype(BF16)
    o_ref[0] = _dot(a, w_ref[0].astype(BF16)) + b_ref[0]


def modulation_all(cond8, w_ada, b_ada):
    n = w_ada.shape[-1]
    tn = 1024
    return pl.pallas_call(
        _mod_kernel,
        grid=(DEPTH, n // tn),
        in_specs=[pl.BlockSpec((8, D_MODEL), lambda l, j: (0, 0)),
                  pl.BlockSpec((1, D_MODEL, tn), lambda l, j: (l, 0, j)),
                  pl.BlockSpec((1, 1, tn), lambda l, j: (l, 0, j))],
        out_specs=pl.BlockSpec((1, 8, tn), lambda l, j: (l, 0, j)),
        out_shape=jax.ShapeDtypeStruct((DEPTH, 8, n), F32),
        compiler_params=_params(("arbitrary", "arbitrary")),
        name="modulation",
    )(cond8, w_ada, b_ada.reshape(DEPTH, 1, n))


def _norm_mod_kernel(x_ref, g_ref, sc_ref, sh_ref, o_ref):
    x = x_ref[...]
    y = x * lax.rsqrt(jnp.mean(x * x, axis=-1, keepdims=True) + NORM_EPS) * g_ref[...]
    o_ref[...] = (y * (1.0 + sc_ref[0]) + sh_ref[0]).astype(o_ref.dtype)


def norm_mod(x, g, sc, sh, tm=512):
    m = x.shape[0]
    cmap = lambda i: (_cond_of_row(i * tm), 0, 0)
    return pl.pallas_call(
        _norm_mod_kernel,
        grid=(m // tm,),
        in_specs=[pl.BlockSpec((tm, D_MODEL), lambda i: (i, 0)),
                  pl.BlockSpec((1, D_MODEL), lambda i: (0, 0)),
                  pl.BlockSpec((1, 1, D_MODEL), cmap),
                  pl.BlockSpec((1, 1, D_MODEL), cmap)],
        out_specs=pl.BlockSpec((tm, D_MODEL), lambda i: (i, 0)),
        out_shape=jax.ShapeDtypeStruct((m, D_MODEL), BF16),
        compiler_params=_params(("arbitrary",)),
        name="norm_mod",
    )(x, g, sc, sh)


def _rmsnorm_kernel(x_ref, g_ref, o_ref):
    x = x_ref[...]
    o_ref[...] = x * lax.rsqrt(jnp.mean(x * x, axis=-1, keepdims=True) + NORM_EPS) * g_ref[...]


def final_norm(x, g, row0, rows, tm=512):
    return pl.pallas_call(
        _rmsnorm_kernel,
        grid=(rows // tm,),
        in_specs=[pl.BlockSpec((tm, D_MODEL), lambda i: (i + row0 // tm, 0)),
                  pl.BlockSpec((1, D_MODEL), lambda i: (0, 0))],
        out_specs=pl.BlockSpec((tm, D_MODEL), lambda i: (i, 0)),
        out_shape=jax.ShapeDtypeStruct((rows, D_MODEL), F32),
        compiler_params=_params(("arbitrary",)),
        name="final_norm",
    )(x, g)


def _mm_kernel(a_ref, w_ref, o_ref, *, act):
    acc = _dot(a_ref[...], w_ref[...])
    if act == "sigmoid":
        acc = jax.nn.sigmoid(acc)
    o_ref[...] = acc.astype(o_ref.dtype)


def matmul(a, w, out_dtype, *, row0=0, rows=None, tm=1024, tn=1024, act=None, name="matmul"):
    k = a.shape[1]
    n = w.shape[1]
    rows = a.shape[0] if rows is None else rows
    tn = min(tn, n)
    assert rows % tm == 0 and row0 % tm == 0 and n % tn == 0
    r0 = row0 // tm
    return pl.pallas_call(
        functools.partial(_mm_kernel, act=act),
        grid=(rows // tm, n // tn),
        in_specs=[pl.BlockSpec((tm, k), lambda i, j: (i + r0, 0)),
                  pl.BlockSpec((k, tn), lambda i, j: (0, j))],
        out_specs=pl.BlockSpec((tm, tn), lambda i, j: (i, j)),
        out_shape=jax.ShapeDtypeStruct((rows, n), out_dtype),
        compiler_params=_params(("arbitrary", "arbitrary")),
        name=name,
    )(a, w)


def _merge_kernel(oa_ref, ob_ref, oc_ref, ga_ref, gb_ref, gc_ref, wa_ref, wb_ref, wc_ref, o_ref):
    y = ga_ref[...].astype(F32) * _dot(oa_ref[...], wa_ref[...])
    y += gb_ref[...].astype(F32) * _dot(ob_ref[...], wb_ref[...])
    y += gc_ref[...].astype(F32) * _dot(oc_ref[...], wc_ref[...])
    o_ref[...] = y.astype(o_ref.dtype)


def branch_merge(oa, ob, oc, gates, wa, wb, wc, tm=1024, tn=1024):
    m = oa.shape[0]
    nb = D_MODEL // tn
    row = lambda i, j: (i, 0)
    col = lambda i, j: (0, j)
    return pl.pallas_call(
        _merge_kernel,
        grid=(m // tm, nb),
        in_specs=[pl.BlockSpec((tm, GLA_V_W), row),
                  pl.BlockSpec((tm, DIFF_V_W), row),
                  pl.BlockSpec((tm, NA_W), row),
                  pl.BlockSpec((tm, tn), lambda i, j: (i, j)),
                  pl.BlockSpec((tm, tn), lambda i, j: (i, j + nb)),
                  pl.BlockSpec((tm, tn), lambda i, j: (i, j + 2 * nb)),
                  pl.BlockSpec((GLA_V_W, tn), col),
                  pl.BlockSpec((DIFF_V_W, tn), col),
                  pl.BlockSpec((NA_W, tn), col)],
        out_specs=pl.BlockSpec((tm, tn), lambda i, j: (i, j)),
        out_shape=jax.ShapeDtypeStruct((m, D_MODEL), BF16),
        compiler_params=_params(("arbitrary", "arbitrary")),
        name="branch_merge",
    )(oa, ob, oc, gates, gates, gates, wa, wb, wc)


def _resid_mm_kernel(a_ref, w_ref, x_ref, gt_ref, o_ref):
    o_ref[...] = x_ref[...] + gt_ref[0] * _dot(a_ref[...], w_ref[...])


def resid_matmul(a, w, x, gt, tm, tn, name):
    m, k = a.shape
    n = w.shape[1]
    return pl.pallas_call(
        _resid_mm_kernel,
        grid=(m // tm, n // tn),
        in_specs=[pl.BlockSpec((tm, k), lambda i, j: (i, 0)),
                  pl.BlockSpec((k, tn), lambda i, j: (0, j)),
                  pl.BlockSpec((tm, tn), lambda i, j: (i, j)),
                  pl.BlockSpec((1, 1, tn), lambda i, j: (_cond_of_row(i * tm), 0, j))],
        out_specs=pl.BlockSpec((tm, tn), lambda i, j: (i, j)),
        out_shape=jax.ShapeDtypeStruct((m, n), F32),
        compiler_params=_params(("arbitrary", "arbitrary")),
        name=name,
    )(a, w, x, gt)


def _ffn_up_kernel(h_ref, wg_ref, wu_ref, o_ref):
    h = h_ref[...]
    g = _dot(h, wg_ref[...])
    u = _dot(h, wu_ref[...])
    o_ref[...] = (g * jax.nn.sigmoid(g) * u).astype(o_ref.dtype)


def ffn_up(h, wg, wu, tm=1024, tn=512):
    m = h.shape[0]
    return pl.pallas_call(
        _ffn_up_kernel,
        grid=(m // tm, D_FF // tn),
        in_specs=[pl.BlockSpec((tm, D_MODEL), lambda i, j: (i, 0)),
                  pl.BlockSpec((D_MODEL, tn), lambda i, j: (0, j)),
                  pl.BlockSpec((D_MODEL, tn), lambda i, j: (0, j))],
        out_specs=pl.BlockSpec((tm, tn), lambda i, j: (i, j)),
        out_shape=jax.ShapeDtypeStruct((m, D_FF), BF16),
        compiler_params=_params(("arbitrary", "arbitrary")),
        name="ffn_up",
    )(h, wg, wu)


def _gla_kernel(q_ref, k_ref, v_ref, gr_ref, lr_ref, wd_ref, bd_ref, g_ref, s0_ref,
                o_ref, sfin_ref, s_scr, of_scr, *, nblk, has_state):
    pss = pl.program_id(1)
    blk = pl.program_id(2)
    C = GLA_CHUNK
    nch = GLA_BLOCK // C

    @pl.when(blk == 0)
    def _init():
        if has_state:
            s_scr[...] = s0_ref[0, 0]
        else:
            s_scr[...] = jnp.zeros_like(s_scr)

    lr = lr_ref[...]
    lr_hi = lr.astype(BF16)
    lr_lo = (lr - lr_hi.astype(F32)).astype(BF16)
    wd = wd_ref[0]
    wd_hi = wd.astype(BF16)
    wd_lo = (wd - wd_hi.astype(F32)).astype(BF16)
    pre = _dot(lr_hi, wd_hi) + _dot(lr_hi, wd_lo) + _dot(lr_lo, wd_hi) + bd_ref[0]
    la_all = jax.nn.log_sigmoid(pre) * (1.0 / GLA_TAU)

    row = lax.broadcasted_iota(jnp.int32, (C, C), 0)
    colm = lax.broadcasted_iota(jnp.int32, (C, C), 1)
    lane = lax.broadcasted_iota(jnp.int32, (1, LANES), 1)
    head_mask = (lane < GLA_DK, lane >= GLA_DK)
    eye = (lax.broadcasted_iota(jnp.int32, (LANES, LANES), 0)
           == lax.broadcasted_iota(jnp.int32, (LANES, LANES), 1))

    def run_direction(backward):
        keep = (colm >= row) if backward else (colm <= row)
        tri = jnp.where(keep, 1.0, 0.0).astype(BF16)
        last, mid = (0, C // 2) if backward else (C - 1, C // 2 - 1)
        order = range(nch - 1, -1, -1) if backward else range(nch)
        for c in order:
            rs = slice(c * C, (c + 1) * C)
            for p in range(GLA_HEADS // 2):
                ls = slice(p * LANES, (p + 1) * LANES)
                la = la_all[rs, ls]
                hi, md, lo = _split3(la)
                b = _dot(tri, hi) + _dot(tri, md) + _dot(tri, lo)
                b_last = b[last:last + 1, :]
                b_mid = b[mid:mid + 1, :]
                q = q_ref[rs, ls] * (GLA_DK ** -0.5)
                k = k_ref[rs, ls]
                q_in = q * jnp.exp(b - b_mid)
                k_in = (k * jnp.exp(b_mid - b)).astype(BF16)
                q_ex = q * jnp.exp(b)
                k_dc = k * jnp.exp(b_last - b)
                dec_col = jnp.exp(jnp.sum(jnp.where(eye, b_last, 0.0), axis=1, keepdims=True))
                s_old = s_scr[p]
                s_old_bf = s_old.astype(BF16)
                s_new = dec_col * s_old
                for h in range(2):
                    hd = 2 * p + h
                    vs = slice(hd * GLA_DV, (hd + 1) * GLA_DV)
                    v = v_ref[rs, vs].astype(BF16)
                    att = _dot_nt(jnp.where(head_mask[h], q_in, 0.0).astype(BF16), k_in)
                    att = jnp.where(keep, att, 0.0).astype(BF16)
                    o = _dot(att, v) + _dot(jnp.where(head_mask[h], q_ex, 0.0).astype(BF16), s_old_bf)
                    s_new = s_new + _dot_tn(jnp.where(head_mask[h], k_dc, 0.0).astype(BF16), v)
                    if backward:
                        tot = of_scr[blk_rows(c), vs] + o
                        y = tot * lax.rsqrt(jnp.mean(tot * tot, axis=-1, keepdims=True) + NORM_EPS) * g_ref[...]
                        gr = gr_ref[rs, vs]
                        o_ref[rs, vs] = (y * (gr * jax.nn.sigmoid(gr))).astype(o_ref.dtype)
                    else:
                        of_scr[blk_rows(c), vs] = o
                s_scr[p] = s_new

    def blk_rows(c):
        pos = jnp.where(pss == 0, blk, nblk - 1 - blk)
        return pl.ds(pl.multiple_of(pos * GLA_BLOCK + c * C, C), C)

    @pl.when(pss == 0)
    def _fwd():
        run_direction(False)

    @pl.when(pss == 1)
    def _bwd():
        run_direction(True)

    @pl.when(blk == nblk - 1)
    def _fin():
        sfin_ref[0, 0] = s_scr[...]


def gla_mixer(pa, wd, bd, g, s0, *, row0, nseq, seqlen):
    nblk = seqlen // GLA_BLOCK
    b0 = row0 // GLA_BLOCK
    has_state = s0 is not None
    if s0 is None:
        s0 = jnp.zeros((2, nseq, 2, LANES, LANES), F32)

    def rb(s, p, b):
        return b0 + s * nblk + b + p * (nblk - 1 - 2 * b)

    def orb(s, p, b):
        return s * nblk + nblk - 1 - p * b

    kern = functools.partial(_gla_kernel, nblk=nblk, has_state=has_state)
    return pl.pallas_call(
        kern,
        grid=(nseq, 2, nblk),
        in_specs=[pl.BlockSpec((GLA_BLOCK, GLA_QK_W), lambda s, p, b: (rb(s, p, b), 0)),
                  pl.BlockSpec((GLA_BLOCK, GLA_QK_W), lambda s, p, b: (rb(s, p, b), 1)),
                  pl.BlockSpec((GLA_BLOCK, GLA_V_W), lambda s, p, b: (rb(s, p, b), 1)),
                  pl.BlockSpec((GLA_BLOCK, GLA_V_W), lambda s, p, b: (rb(s, p, b), 2)),
                  pl.BlockSpec((GLA_BLOCK, LANES), lambda s, p, b: (rb(s, p, b), GLA_A_W // LANES - 1)),
                  pl.BlockSpec((1, LANES, GLA_QK_W), lambda s, p, b: (p, 0, 0)),
                  pl.BlockSpec((1, 1, GLA_QK_W), lambda s, p, b: (p, 0, 0)),
                  pl.BlockSpec((1, GLA_DV), lambda s, p, b: (0, 0)),
                  pl.BlockSpec((1, 1, 2, LANES, LANES), lambda s, p, b: (p, s, 0, 0, 0))],
        out_specs=[pl.BlockSpec((GLA_BLOCK, GLA_V_W), lambda s, p, b: (orb(s, p, b), 0)),
                   pl.BlockSpec((1, 1, 2, LANES, LANES), lambda s, p, b: (p, s, 0, 0, 0))],
        out_shape=[jax.ShapeDtypeStruct((nseq * seqlen, GLA_V_W), BF16),
                   jax.ShapeDtypeStruct((2, nseq, 2, LANES, LANES), F32)],
        scratch_shapes=[pltpu.VMEM((2, LANES, LANES), F32),
                        pltpu.VMEM((seqlen, GLA_V_W), F32)],
        compiler_params=_params(("arbitrary", "arbitrary", "arbitrary")),
        name="gla_mixer",
    )(pa, pa, pa, pa, pa, wd, bd, g, s0)


def _rope_kernel(x_ref, cos_ref, sin_ref, o_ref):
    cos = cos_ref[...]
    sin = sin_ref[...]
    lane = lax.broadcasted_iota(jnp.int32, (1, LANES), 1)
    first = (lane % (2 * ROPE_NFREQ)) < ROPE_NFREQ
    nq = DIFF_QK_W // LANES
    for j in range(2 * nq):
        x = x_ref[:, j * LANES:(j + 1) * LANES].astype(F32)
        partner = jnp.where(first, pltpu.roll(x, LANES - ROPE_NFREQ, 1), pltpu.roll(x, ROPE_NFREQ, 1))
        y = x * cos + partner * sin
        if j < nq:
            y = y * (DIFF_DK ** -0.5)
        o_ref[:, j * LANES:(j + 1) * LANES] = y.astype(o_ref.dtype)


def rope_qk(pb_dec, cos_t, sin_t, tm=512):
    nt = DEC_SEQ // tm
    return pl.pallas_call(
        _rope_kernel,
        grid=(M_DEC // tm,),
        in_specs=[pl.BlockSpec((tm, 2 * DIFF_QK_W), lambda i: (i, 0)),
                  pl.BlockSpec((tm, LANES), lambda i: (i % nt, 0)),
                  pl.BlockSpec((tm, LANES), lambda i: (i % nt, 0))],
        out_specs=pl.BlockSpec((tm, 2 * DIFF_QK_W), lambda i: (i, 0)),
        out_shape=jax.ShapeDtypeStruct((M_DEC, 2 * DIFF_QK_W), BF16),
        compiler_params=_params(("arbitrary",)),
        name="rope_qk",
    )(pb_dec, cos_t, sin_t)


def _diff_kernel(lam_ref, q_ref, kn_ref, vn_ref, *rest, has_cache, q_scale, lam_init):
    if has_cache:
        kc_ref, vc_ref, g_ref, o_ref, vaug, kc_scr = rest
    else:
        g_ref, o_ref, vaug = rest
    n_new = kn_ref.shape[0]
    n_c = PAST_LEN if has_cache else 0
    qb = pl.program_id(2)

    @pl.when(qb == 0)
    def _stage():
        vaug[:, LANES:] = jnp.ones((n_c + n_new, LANES), BF16)
        vaug[n_c:, :LANES] = vn_ref[...].astype(BF16)
        if has_cache:
            vaug[:n_c, :LANES] = vc_ref[0].astype(BF16)
            kc_scr[...] = kc_ref[0].astype(BF16)

    q = q_ref[...]
    if q_scale != 1.0:
        q = q * q_scale
    q = q.astype(BF16)
    kn = kn_ref[...].astype(BF16)
    lane = lax.broadcasted_iota(jnp.int32, (1, LANES), 1)
    outs = []
    for mp in range(2):
        msk = (lane < DIFF_DK) if mp == 0 else (lane >= DIFF_DK)
        qm = jnp.where(msk, q, jnp.zeros_like(q))
        s_n = _dot_nt(qm, kn)
        m = jnp.max(s_n, axis=-1, keepdims=True)
        if has_cache:
            s_c = _dot_nt(qm, kc_scr[...])
            m = jnp.maximum(m, jnp.max(s_c, axis=-1, keepdims=True))
            acc = _dot(jnp.exp(s_c - m).astype(BF16), vaug[:n_c, :])
            acc += _dot(jnp.exp(s_n - m).astype(BF16), vaug[n_c:, :])
        else:
            acc = _dot(jnp.exp(s_n - m).astype(BF16), vaug[...])
        outs.append(acc[:, :LANES] / acc[:, LANES:])
    o = outs[0] - lam_ref[0] * outs[1]
    y = o * lax.rsqrt(jnp.mean(o * o, axis=-1, keepdims=True) + NORM_EPS) * g_ref[...]
    o_ref[...] = (y * (1.0 - lam_init)).astype(o_ref.dtype)


def diff_attention_ctx(pb_ctx, lam, g, lam_init):
    kern = functools.partial(_diff_kernel, has_cache=False, q_scale=DIFF_DK ** -0.5, lam_init=lam_init)
    H = DIFF_HEADS
    return pl.pallas_call(
        kern,
        grid=(BATCH, H, 1),
        in_specs=[pl.BlockSpec(memory_space=pltpu.SMEM),
                  pl.BlockSpec((SEQ, LANES), lambda s, h, b: (s, h)),
                  pl.BlockSpec((SEQ, LANES), lambda s, h, b: (s, H + h)),
                  pl.BlockSpec((SEQ, LANES), lambda s, h, b: (s, 2 * H + h)),
                  pl.BlockSpec((1, DIFF_DV), lambda s, h, b: (0, 0))],
        out_specs=pl.BlockSpec((SEQ, LANES), lambda s, h, b: (s, h)),
        out_shape=jax.ShapeDtypeStruct((M_CTX, DIFF_V_W), BF16),
        scratch_shapes=[pltpu.VMEM((SEQ, 2 * LANES), BF16)],
        compiler_params=_params(("arbitrary", "arbitrary", "arbitrary")),
        name="diff_attention_ctx",
    )(lam, pb_ctx, pb_ctx, pb_ctx, g)


def diff_attention_dec(qk_rope, pb_dec, cache_k, cache_v, lam, g, lam_init, tq=256):
    kern = functools.partial(_diff_kernel, has_cache=True, q_scale=1.0, lam_init=lam_init)
    H = DIFF_HEADS
    nqb = DEC_SEQ // tq
    ntot = PAST_LEN + DEC_SEQ
    return pl.pallas_call(
        kern,
        grid=(DEC_BATCH, H, nqb),
        in_specs=[pl.BlockSpec(memory_space=pltpu.SMEM),
                  pl.BlockSpec((tq, LANES), lambda s, h, b: (s * nqb + b, h)),
                  pl.BlockSpec((DEC_SEQ, LANES), lambda s, h, b: (s, H + h)),
                  pl.BlockSpec((DEC_SEQ, LANES), lambda s, h, b: (s, 2 * H + h)),
                  pl.BlockSpec((1, PAST_LEN, LANES), lambda s, h, b: (s, 0, h)),
                  pl.BlockSpec((1, PAST_LEN, LANES), lambda s, h, b: (s, 0, h)),
                  pl.BlockSpec((1, DIFF_DV), lambda s, h, b: (0, 0))],
        out_specs=pl.BlockSpec((tq, LANES), lambda s, h, b: (s * nqb + b, h)),
        out_shape=jax.ShapeDtypeStruct((M_DEC, DIFF_V_W), BF16),
        scratch_shapes=[pltpu.VMEM((ntot, 2 * LANES), BF16),
                        pltpu.VMEM((PAST_LEN, LANES), BF16)],
        compiler_params=_params(("arbitrary", "arbitrary", "arbitrary")),
        name="diff_attention_dec",
    )(lam, qk_rope, qk_rope, pb_dec, cache_k, cache_v, g)


def _softmax_pair(q, keys_vals, biases):
    lane = lax.broadcasted_iota(jnp.int32, (1, LANES), 1)
    outs = []
    for h in range(2):
        msk = (lane < NA_DK) if h == 0 else (lane >= NA_DK)
        qm = jnp.where(msk, q, jnp.zeros_like(q))
        ss = []
        for (k, _), bias in zip(keys_vals, biases):
            s = _dot_nt(qm, k)
            if bias is not None:
                s = s + bias(h)
            ss.append(s)
        m = functools.reduce(jnp.maximum, [jnp.max(s, axis=-1, keepdims=True) for s in ss])
        num = 0.0
        den = 0.0
        for s, (_, v) in zip(ss, keys_vals):
            e = jnp.exp(s - m)
            den = den + jnp.sum(e, axis=-1, keepdims=True)
            num = num + _dot(e.astype(BF16), v)
        outs.append(num / den)
    return jnp.where(lane < NA_DK, outs[0], outs[1])


def _ctx_attn_kernel(q_ref, k_ref, v_ref, o_ref):
    q = (q_ref[...] * (NA_DK ** -0.5)).astype(BF16)
    k = k_ref[...].astype(BF16)
    v = v_ref[...].astype(BF16)
    o_ref[...] = _softmax_pair(q, [(k, v)], [None]).astype(o_ref.dtype)


def context_attention_ctx(pc_ctx):
    hp = NA_HEADS // 2
    return pl.pallas_call(
        _ctx_attn_kernel,
        grid=(BATCH, hp),
        in_specs=[pl.BlockSpec((SEQ, LANES), lambda s, p: (s, p)),
                  pl.BlockSpec((SEQ, LANES), lambda s, p: (s, hp + p)),
                  pl.BlockSpec((SEQ, LANES), lambda s, p: (s, 2 * hp + p))],
        out_specs=pl.BlockSpec((SEQ, LANES), lambda s, p: (s, p)),
        out_shape=jax.ShapeDtypeStruct((M_CTX, NA_W), BF16),
        compiler_params=_params(("arbitrary", "arbitrary")),
        name="context_attention",
    )(pc_ctx, pc_ctx, pc_ctx)


def _na_window_start(rb):
    return jnp.clip(NA_QROWS * rb - NA_WIN_H // 2, 0, GRID_H - NA_KROWS)


def _na_kernel(q_ref, k_ref, v_ref, kc_ref, vc_ref, bias_ref, o_ref):
    rb = pl.program_id(2)
    start = pl.multiple_of(_na_window_start(rb) * GRID_W, GRID_W)
    kl = k_ref[pl.ds(start, NA_NLOC), :]
    vl = v_ref[pl.ds(start, NA_NLOC), :]
    kc = kc_ref[0].astype(BF16)
    vc = vc_ref[0].astype(BF16)
    q = q_ref[...] * (NA_DK ** -0.5)
    o = _softmax_pair(q, [(kl, vl), (kc, vc)], [lambda h: bias_ref[0, h], None])
    o_ref[...] = o.astype(o_ref.dtype)


def _na_situation(rb):
    nrb = GRID_H // NA_QROWS
    return jnp.where(rb < 2, rb, jnp.where(rb >= nrb - 2, rb - (nrb - 5), 2))


def neighbourhood_attention_dec(pc_dec, cache_k, cache_v, bias):
    hp = NA_HEADS // 2
    nrb = GRID_H // NA_QROWS
    return pl.pallas_call(
        _na_kernel,
        grid=(DEC_BATCH, hp, nrb),
        in_specs=[pl.BlockSpec((NA_QBLK, LANES), lambda s, p, r: (s * nrb + r, p)),
                  pl.BlockSpec((DEC_SEQ, LANES), lambda s, p, r: (s, hp + p)),
                  pl.BlockSpec((DEC_SEQ, LANES), lambda s, p, r: (s, 2 * hp + p)),
                  pl.BlockSpec((1, PAST_LEN, LANES), lambda s, p, r: (s, 0, p)),
                  pl.BlockSpec((1, PAST_LEN, LANES), lambda s, p, r: (s, 0, p)),
                  pl.BlockSpec((1, 2, NA_QBLK, NA_NLOC), lambda s, p, r: (_na_situation(r), p, 0, 0))],
        out_specs=pl.BlockSpec((NA_QBLK, LANES), lambda s, p, r: (s * nrb + r, p)),
        out_shape=jax.ShapeDtypeStruct((M_DEC, NA_W), BF16),
        compiler_params=_params(("arbitrary", "arbitrary", "arbitrary")),
        name="neighbourhood_attention",
    )(pc_dec, pc_dec, pc_dec, cache_k, cache_v, bias)


def _na_bias_tables(rpb):
    nrb = GRID_H // NA_QROWS
    rbs = [0, 1, 2, nrb - 2, nrb - 1]
    qc = np.arange(GRID_W)
    kc = np.arange(GRID_W)
    col_start = np.clip(qc - NA_WIN_W // 2, 0, GRID_W - NA_WIN_W)
    col_ok = (kc[None, :] >= col_start[:, None]) & (kc[None, :] < col_start[:, None] + NA_WIN_W)
    dcol = np.clip(kc[None, :] - qc[:, None], 1 - NA_WIN_W, NA_WIN_W - 1) + NA_WIN_W - 1
    drow_all, ok_all = [], []
    for rb in rbs:
        ws = int(np.clip(NA_QROWS * rb - NA_WIN_H // 2, 0, GRID_H - NA_KROWS))
        qr = NA_QROWS * rb + np.arange(NA_QROWS)
        kr = ws + np.arange(NA_KROWS)
        win = np.clip(qr - NA_WIN_H // 2, 0, GRID_H - NA_WIN_H)
        row_ok = (kr[None, :] >= win[:, None]) & (kr[None, :] < win[:, None] + NA_WIN_H)
        drow = np.clip(kr[None, :] - qr[:, None] + NA_WIN_H - 1, 0, 2 * NA_WIN_H - 2)
        ok = row_ok[:, None, :, None] & col_ok[None, :, None, :]
        drow_all.append(np.broadcast_to(drow[:, None, :, None], ok.shape))
        ok_all.append(ok)
    drow_i = np.stack(drow_all).reshape(5, NA_QBLK, NA_NLOC)
    dcol_i = np.broadcast_to(dcol[None, None, :, None, :],
                             (5, NA_QROWS, GRID_W, NA_KROWS, GRID_W)).reshape(5, NA_QBLK, NA_NLOC)
    ok = np.stack(ok_all).reshape(5, NA_QBLK, NA_NLOC)
    tab = rpb.astype(F32)[:, drow_i, dcol_i]
    tab = jnp.where(ok[None], tab, NEG_INF)
    return jnp.transpose(tab, (1, 0, 2, 3))


def _rope_tables():
    pos = np.arange(DEC_SEQ)
    row = (pos // GRID_W).astype(np.float32)
    col = (pos % GRID_W).astype(np.float32)
    inv = jnp.asarray(ROPE_BASE, F32) ** (-jnp.arange(ROPE_NFREQ, dtype=F32) / ROPE_NFREQ)
    ang_r = jnp.asarray(row)[:, None] * inv
    ang_c = jnp.asarray(col)[:, None] * inv
    cos64 = jnp.concatenate([jnp.cos(ang_r), jnp.cos(ang_r), jnp.cos(ang_c), jnp.cos(ang_c)], axis=1)
    sin64 = jnp.concatenate([-jnp.sin(ang_r), jnp.sin(ang_r), -jnp.sin(ang_c), jnp.sin(ang_c)], axis=1)
    return jnp.tile(cos64, (1, 2)), jnp.tile(sin64, (1, 2))


def kernel(x_prompt, x_sample, c, cache_diff_k, cache_diff_v, cache_na_k, cache_na_v, state_gla_fwd,
           state_gla_bwd, c_ctx, w_ada, b_ada, norm_mix_g, w_in, w_decay, b_decay, gla_norm_g, diff_lambda,
           diff_norm_g, na_rpb, w_br_a, w_br_b, w_br_c, w_out, norm_ffn_g, w_ffn_gate, w_ffn_up, w_ffn_down,
           norm_final_g):
    D = D_MODEL
    x = jnp.concatenate([x_prompt.reshape(M_CTX, D), x_sample.reshape(M_DEC, D)], axis=0)
    cond8 = jnp.concatenate([c_ctx[None], c, jnp.zeros((8 - N_COND, D), F32)], axis=0)
    mod = modulation_all(cond8, w_ada, b_ada)
    cos_t, sin_t = _rope_tables()

    offs = np.cumsum((0,) + IN_WIDTHS)
    a_end, b_end, c_end = offs[5], offs[8], offs[11]
    new_state = []
    for l in range(DEPTH):
        lam_init = 0.8 - 0.6 * float(np.exp(-0.3 * l))
        m6 = mod[l, :N_COND].reshape(N_COND, 6, 1, D)
        sh_a, sc_a, gt_a, sh_f, sc_f, gt_f = (m6[:, i] for i in range(6))
        wl = w_in[l]
        w_a = jnp.concatenate([wl[:, :a_end], jnp.zeros((D, GLA_A_W - a_end), F32)], axis=1).astype(BF16)
        w_b = wl[:, a_end:b_end].astype(BF16)
        w_c = wl[:, b_end:c_end].astype(BF16)
        w_g = wl[:, c_end:].astype(BF16)

        h = norm_mod(x, norm_mix_g[l][None], sc_a, sh_a)
        pa = matmul(h, w_a, F32, tm=512, tn=GLA_A_W, name="in_proj_gla")
        pb_ctx = matmul(h, w_b, F32, rows=M_CTX, name="in_proj_diff_ctx")
        pb_dec = matmul(h, w_b, BF16, row0=M_CTX, rows=M_DEC, name="in_proj_diff_dec")
        pc_ctx = matmul(h, w_c, F32, rows=M_CTX, tn=768, name="in_proj_na_ctx")
        pc_dec = matmul(h, w_c, BF16, row0=M_CTX, rows=M_DEC, tn=768, name="in_proj_na_dec")
        gates = matmul(h, w_g, BF16, act="sigmoid", name="in_proj_gates")

        wd = jnp.zeros((2, LANES, GLA_QK_W), F32)
        wd = wd.at[0, :GLA_LR].set(w_decay[l, 0]).at[1, GLA_LR:2 * GLA_LR].set(w_decay[l, 1])
        bd = b_decay[l][:, None, :]
        gn = gla_norm_g[l][None]
        s0 = jnp.stack([state_gla_fwd[:, l], state_gla_bwd[:, l]]).reshape(2, DEC_BATCH, 2, LANES, LANES)
        oa_ctx, s_fin = gla_mixer(pa, wd, bd, gn, None, row0=0, nseq=BATCH, seqlen=SEQ)
        oa_dec, _ = gla_mixer(pa, wd, bd, gn, s0, row0=M_CTX, nseq=DEC_BATCH, seqlen=DEC_SEQ)
        oa = jnp.concatenate([oa_ctx, oa_dec], axis=0)

        lp = diff_lambda[l].astype(F32)
        lam = (jnp.exp(jnp.sum(lp[0] * lp[1])) - jnp.exp(jnp.sum(lp[2] * lp[3])) + lam_init).reshape(1)
        dg = diff_norm_g[l][None]
        ob_ctx = diff_attention_ctx(pb_ctx, lam, dg, lam_init)
        qk_rope = rope_qk(pb_dec, cos_t, sin_t)
        ob_dec = diff_attention_dec(qk_rope, pb_dec,
                                    cache_diff_k[:, l].reshape(DEC_BATCH, PAST_LEN, DIFF_QK_W),
                                    cache_diff_v[:, l].reshape(DEC_BATCH, PAST_LEN, DIFF_V_W),
                                    lam, dg, lam_init)
        ob = jnp.concatenate([ob_ctx, ob_dec], axis=0)

        oc_ctx = context_attention_ctx(pc_ctx)
        oc_dec = neighbourhood_attention_dec(pc_dec,
                                             cache_na_k[:, l].reshape(DEC_BATCH, PAST_LEN, NA_W),
                                             cache_na_v[:, l].reshape(DEC_BATCH, PAST_LEN, NA_W),
                                             _na_bias_tables(na_rpb[l]))
        oc = jnp.concatenate([oc_ctx, oc_dec], axis=0)

        merged = branch_merge(oa, ob, oc, gates, w_br_a[l].astype(BF16), w_br_b[l].astype(BF16),
                              w_br_c[l].astype(BF16))
        x = resid_matmul(merged, w_out[l].astype(BF16), x, gt_a, tm=1024, tn=1024, name="out_proj")

        h2 = norm_mod(x, norm_ffn_g[l][None], sc_f, sh_f)
        act = ffn_up(h2, w_ffn_gate[l].astype(BF16), w_ffn_up[l].astype(BF16))
        x = resid_matmul(act, w_ffn_down[l].astype(BF16), x, gt_f, tm=512, tn=512, name="ffn_down")

        new_state.append((
            pb_ctx[:, DIFF_QK_W:2 * DIFF_QK_W].reshape(BATCH, SEQ, DIFF_HEADS, 2 * DIFF_DK),
            pb_ctx[:, 2 * DIFF_QK_W:].reshape(BATCH, SEQ, DIFF_HEADS, DIFF_DV),
            pc_ctx[:, NA_W:2 * NA_W].reshape(BATCH, SEQ, NA_HEADS, NA_DK),
            pc_ctx[:, 2 * NA_W:].reshape(BATCH, SEQ, NA_HEADS, NA_DK),
            s_fin[0].reshape(BATCH, GLA_HEADS, GLA_DK, GLA_DV),
            s_fin[1].reshape(BATCH, GLA_HEADS, GLA_DK, GLA_DV)))

    gfin = norm_final_g[None]
    y_prompt = final_norm(x, gfin, 0, M_CTX).reshape(BATCH, SEQ, D)
    y_sample = final_norm(x, gfin, M_CTX, M_DEC).reshape(DEC_BATCH, DEC_SEQ, D)
    stacked = tuple(jnp.stack([st[i] for st in new_state], axis=1) for i in range(6))
    return (y_prompt, y_sample) + stacked
```

```python
import functools

import numpy as np
import jax
import jax.numpy as jnp
from jax import lax
from jax.experimental import pallas as pl
from jax.experimental.pallas import tpu as pltpu

F32 = jnp.float32
BF16 = jnp.bfloat16

D_MODEL = 2048
BATCH = 16
SEQ = 256
DEPTH = 2
DEC_BATCH = 2
DEC_SEQ = 4096
PAST_LEN = 512
GRID_W = 64
GRID_H = DEC_SEQ // GRID_W
NORM_EPS = 1e-6
NEG_INF = -1e30
GLA_HEADS = 4
GLA_DK = 64
GLA_DV = 128
GLA_LR = 16
GLA_TAU = 16.0
GLA_CHUNK = 64
GLA_QK_W = GLA_HEADS * GLA_DK
GLA_V_W = GLA_HEADS * GLA_DV
DIFF_HEADS = 8
DIFF_DK = 64
DIFF_DV = 128
DIFF_QK_W = DIFF_HEADS * 2 * DIFF_DK
DIFF_V_W = DIFF_HEADS * DIFF_DV
ROPE_BASE = 10000.0
ROPE_NFREQ = DIFF_DK // 4
NA_HEADS = 8
NA_DK = 64
NA_W = NA_HEADS * NA_DK
NA_WIN_H = 8
NA_WIN_W = 16
D_FF = ((8 * D_MODEL // 3 + 255) // 256) * 256
IN_WIDTHS = (GLA_QK_W, GLA_QK_W, GLA_V_W, GLA_V_W, 2 * GLA_LR,
             DIFF_QK_W, DIFF_QK_W, DIFF_V_W, NA_W, NA_W, NA_W, 3 * D_MODEL)

M_CTX = BATCH * SEQ
M_DEC = DEC_BATCH * DEC_SEQ
M_ALL = M_CTX + M_DEC
N_COND = 1 + DEC_BATCH

LANES = 128
VMEM_LIMIT_BYTES = 56 * 1024 * 1024

GLA_A_W = 13 * LANES
GLA_BLOCK = 256
NA_QROWS = 2
NA_QBLK = NA_QROWS * GRID_W
NA_KROWS = NA_WIN_H + NA_QROWS - 1
NA_NLOC = NA_KROWS * GRID_W
DIFF_KCHUNK = 512
LOG2E = 1.4426950408889634
DIFF_QSCALE = DIFF_DK ** -0.5 * LOG2E


def _params(sem):
    return pltpu.CompilerParams(dimension_semantics=sem, vmem_limit_bytes=VMEM_LIMIT_BYTES)


def _cond_of_row(row):
    return jnp.where(row < M_CTX, 0, 1 + (row - M_CTX) // DEC_SEQ)


def _dot(a, b):
    return jnp.dot(a, b, preferred_element_type=F32)


def _dot_nt(a, b):
    return lax.dot_general(a, b, (((1,), (1,)), ((), ())), preferred_element_type=F32)


def _dot_tn(a, b):
    return lax.dot_general(a, b, (((0,), (0,)), ((), ())), preferred_element_type=F32)


def _split3(x):
    hi = x.astype(BF16)
    r1 = x - hi.astype(F32)
    mid = r1.astype(BF16)
    lo = (r1 - mid.astype(F32)).astype(BF16)
    return hi, mid, lo


def _mod_kernel(c_ref, w_ref, b_ref, o_ref):
    c = c_ref[...]
    a = (c * jax.nn.sigmoid(c)).astype(BF16)
    o_ref[0] = _dot(a, w_ref[0].astype(BF16)) + b_ref[0]


def modulation_all(cond8, w_ada, b_ada):
    n = w_ada.shape[-1]
    tn = 1024
    return pl.pallas_call(
        _mod_kernel,
        grid=(DEPTH, n // tn),
        in_specs=[pl.BlockSpec((8, D_MODEL), lambda l, j: (0, 0)),
                  pl.BlockSpec((1, D_MODEL, tn), lambda l, j: (l, 0, j)),
                  pl.BlockSpec((1, 1, tn), lambda l, j: (l, 0, j))],
        out_specs=pl.BlockSpec((1, 8, tn), lambda l, j: (l, 0, j)),
        out_shape=jax.ShapeDtypeStruct((DEPTH, 8, n), F32),
        compiler_params=_params(("arbitrary", "arbitrary")),
        name="modulation",
    )(cond8, w_ada, b_ada.reshape(DEPTH, 1, n))


def _norm_mod_kernel(x_ref, g_ref, sc_ref, sh_ref, o_ref):
    x = x_ref[...]
    y = x * lax.rsqrt(jnp.mean(x * x, axis=-1, keepdims=True) + NORM_EPS) * g_ref[...]
    o_ref[...] = (y * (1.0 + sc_ref[0]) + sh_ref[0]).astype(o_ref.dtype)


def norm_mod(x, g, sc, sh, tm=512):
    m = x.shape[0]
    cmap = lambda i: (_cond_of_row(i * tm), 0, 0)
    return pl.pallas_call(
        _norm_mod_kernel,
        grid=(m // tm,),
        in_specs=[pl.BlockSpec((tm, D_MODEL), lambda i: (i, 0)),
                  pl.BlockSpec((1, D_MODEL), lambda i: (0, 0)),
                  pl.BlockSpec((1, 1, D_MODEL), cmap),
                  pl.BlockSpec((1, 1, D_MODEL), cmap)],
        out_specs=pl.BlockSpec((tm, D_MODEL), lambda i: (i, 0)),
        out_shape=jax.ShapeDtypeStruct((m, D_MODEL), BF16),
        compiler_params=_params(("arbitrary",)),
        name="norm_mod",
    )(x, g, sc, sh)


def _rmsnorm_kernel(x_ref, g_ref, o_ref):
    x = x_ref[...]
    o_ref[...] = x * lax.rsqrt(jnp.mean(x * x, axis=-1, keepdims=True) + NORM_EPS) * g_ref[...]


def final_norm(x, g, row0, rows, tm=512):
    return pl.pallas_call(
        _rmsnorm_kernel,
        grid=(rows // tm,),
        in_specs=[pl.BlockSpec((tm, D_MODEL), lambda i: (i + row0 // tm, 0)),
                  pl.BlockSpec((1, D_MODEL), lambda i: (0, 0))],
        out_specs=pl.BlockSpec((tm, D_MODEL), lambda i: (i, 0)),
        out_shape=jax.ShapeDtypeStruct((rows, D_MODEL), F32),
        compiler_params=_params(("arbitrary",)),
        name="final_norm",
    )(x, g)


def _mm_kernel(a_ref, w_ref, o_ref, *, act):
    acc = _dot(a_ref[...], w_ref[...])
    if act == "sigmoid":
        acc = jax.nn.sigmoid(acc)
    o_ref[...] = acc.astype(o_ref.dtype)


def matmul(a, w, out_dtype, *, row0=0, rows=None, tm=1024, tn=1024, act=None, name="matmul"):
    k = a.shape[1]
    n = w.shape[1]
    rows = a.shape[0] if rows is None else rows
    tn = min(tn, n)
    assert rows % tm == 0 and row0 % tm == 0 and n % tn == 0
    r0 = row0 // tm
    return pl.pallas_call(
        functools.partial(_mm_kernel, act=act),
        grid=(rows // tm, n // tn),
        in_specs=[pl.BlockSpec((tm, k), lambda i, j: (i + r0, 0)),
                  pl.BlockSpec((k, tn), lambda i, j: (0, j))],
        out_specs=pl.BlockSpec((tm, tn), lambda i, j: (i, j)),
        out_shape=jax.ShapeDtypeStruct((rows, n), out_dtype),
        compiler_params=_params(("arbitrary", "arbitrary")),
        name=name,
    )(a, w)


def _merge_kernel(oa_ref, ob_ref, oc_ref, ga_ref, gb_ref, gc_ref, wa_ref, wb_ref, wc_ref, o_ref):
    y = ga_ref[...].astype(F32) * _dot(oa_ref[...], wa_ref[...])
    y += gb_ref[...].astype(F32) * _dot(ob_ref[...], wb_ref[...])
    y += gc_ref[...].astype(F32) * _dot(oc_ref[...], wc_ref[...])
    o_ref[...] = y.astype(o_ref.dtype)


def branch_merge(oa, ob, oc, gates, wa, wb, wc, tm=1024, tn=1024):
    m = oa.shape[0]
    nb = D_MODEL // tn
    row = lambda i, j: (i, 0)
    col = lambda i, j: (0, j)
    return pl.pallas_call(
        _merge_kernel,
        grid=(m // tm, nb),
        in_specs=[pl.BlockSpec((tm, GLA_V_W), row),
                  pl.BlockSpec((tm, DIFF_V_W), row),
                  pl.BlockSpec((tm, NA_W), row),
                  pl.BlockSpec((tm, tn), lambda i, j: (i, j)),
                  pl.BlockSpec((tm, tn), lambda i, j: (i, j + nb)),
                  pl.BlockSpec((tm, tn), lambda i, j: (i, j + 2 * nb)),
                  pl.BlockSpec((GLA_V_W, tn), col),
                  pl.BlockSpec((DIFF_V_W, tn), col),
                  pl.BlockSpec((NA_W, tn), col)],
        out_specs=pl.BlockSpec((tm, tn), lambda i, j: (i, j)),
        out_shape=jax.ShapeDtypeStruct((m, D_MODEL), BF16),
        compiler_params=_params(("arbitrary", "arbitrary")),
        name="branch_merge",
    )(oa, ob, oc, gates, gates, gates, wa, wb, wc)


def _resid_mm_kernel(a_ref, w_ref, x_ref, gt_ref, o_ref):
    o_ref[...] = x_ref[...] + gt_ref[0] * _dot(a_ref[...], w_ref[...])


def resid_matmul(a, w, x, gt, tm, tn, name):
    m, k = a.shape
    n = w.shape[1]
    return pl.pallas_call(
        _resid_mm_kernel,
        grid=(m // tm, n // tn),
        in_specs=[pl.BlockSpec((tm, k), lambda i, j: (i, 0)),
                  pl.BlockSpec((k, tn), lambda i, j: (0, j)),
                  pl.BlockSpec((tm, tn), lambda i, j: (i, j)),
                  pl.BlockSpec((1, 1, tn), lambda i, j: (_cond_of_row(i * tm), 0, j))],
        out_specs=pl.BlockSpec((tm, tn), lambda i, j: (i, j)),
        out_shape=jax.ShapeDtypeStruct((m, n), F32),
        compiler_params=_params(("arbitrary", "arbitrary")),
        name=name,
    )(a, w, x, gt)


def _ffn_up_kernel(h_ref, wg_ref, wu_ref, o_ref):
    h = h_ref[...]
    g = _dot(h, wg_ref[...])
    u = _dot(h, wu_ref[...])
    o_ref[...] = (g * jax.nn.sigmoid(g) * u).astype(o_ref.dtype)


def ffn_up(h, wg, wu, tm=1024, tn=512):
    m = h.shape[0]
    return pl.pallas_call(
        _ffn_up_kernel,
        grid=(m // tm, D_FF // tn),
        in_specs=[pl.BlockSpec((tm, D_MODEL), lambda i, j: (i, 0)),
                  pl.BlockSpec((D_MODEL, tn), lambda i, j: (0, j)),
                  pl.BlockSpec((D_MODEL, tn), lambda i, j: (0, j))],
        out_specs=pl.BlockSpec((tm, tn), lambda i, j: (i, j)),
        out_shape=jax.ShapeDtypeStruct((m, D_FF), BF16),
        compiler_params=_params(("arbitrary", "arbitrary")),
        name="ffn_up",
    )(h, wg, wu)


def _gla_kernel(q_ref, k_ref, v_ref, gr_ref, lr_ref, wd_ref, bd_ref, g_ref, s0_ref,
                o_ref, sfin_ref, s_scr, of_scr, *, nblk, has_state):
    pss = pl.program_id(1)
    blk = pl.program_id(2)
    C = GLA_CHUNK
    nch = GLA_BLOCK // C

    @pl.when(blk == 0)
    def _init():
        if has_state:
            s_scr[...] = s0_ref[0, 0]
        else:
            s_scr[...] = jnp.zeros_like(s_scr)

    lr = lr_ref[...]
    lr_hi = lr.astype(BF16)
    lr_lo = (lr - lr_hi.astype(F32)).astype(BF16)
    wd = wd_ref[0]
    wd_hi = wd.astype(BF16)
    wd_lo = (wd - wd_hi.astype(F32)).astype(BF16)
    pre = _dot(lr_hi, wd_hi) + _dot(lr_hi, wd_lo) + _dot(lr_lo, wd_hi) + bd_ref[0]
    la_all = jax.nn.log_sigmoid(pre) * (1.0 / GLA_TAU)

    row = lax.broadcasted_iota(jnp.int32, (C, C), 0)
    colm = lax.broadcasted_iota(jnp.int32, (C, C), 1)
    lane = lax.broadcasted_iota(jnp.int32, (1, LANES), 1)
    head_mask = (lane < GLA_DK, lane >= GLA_DK)
    eye = (lax.broadcasted_iota(jnp.int32, (LANES, LANES), 0)
           == lax.broadcasted_iota(jnp.int32, (LANES, LANES), 1))

    def run_direction(backward):
        keep = (colm >= row) if backward else (colm <= row)
        tri = jnp.where(keep, 1.0, 0.0).astype(BF16)
        last, mid = (0, C // 2) if backward else (C - 1, C // 2 - 1)
        order = range(nch - 1, -1, -1) if backward else range(nch)
        for c in order:
            rs = slice(c * C, (c + 1) * C)
            for p in range(GLA_HEADS // 2):
                ls = slice(p * LANES, (p + 1) * LANES)
                la = la_all[rs, ls]
                hi, md, lo = _split3(la)
                b = _dot(tri, hi) + _dot(tri, md) + _dot(tri, lo)
                b_last = b[last:last + 1, :]
                b_mid = b[mid:mid + 1, :]
                q = q_ref[rs, ls] * (GLA_DK ** -0.5)
                k = k_ref[rs, ls]
                q_in = q * jnp.exp(b - b_mid)
                k_in = (k * jnp.exp(b_mid - b)).astype(BF16)
                q_ex = q * jnp.exp(b)
                k_dc = k * jnp.exp(b_last - b)
                dec_col = jnp.exp(jnp.sum(jnp.where(eye, b_last, 0.0), axis=1, keepdims=True))
                s_old = s_scr[p]
                s_old_bf = s_old.astype(BF16)
                s_new = dec_col * s_old
                for h in range(2):
                    hd = 2 * p + h
                    vs = slice(hd * GLA_DV, (hd + 1) * GLA_DV)
                    v = v_ref[rs, vs].astype(BF16)
                    att = _dot_nt(jnp.where(head_mask[h], q_in, 0.0).astype(BF16), k_in)
                    att = jnp.where(keep, att, 0.0).astype(BF16)
                    o = _dot(att, v) + _dot(jnp.where(head_mask[h], q_ex, 0.0).astype(BF16), s_old_bf)
                    s_new = s_new + _dot_tn(jnp.where(head_mask[h], k_dc, 0.0).astype(BF16), v)
                    if backward:
                        tot = of_scr[blk_rows(c), vs] + o
                        y = tot * lax.rsqrt(jnp.mean(tot * tot, axis=-1, keepdims=True) + NORM_EPS) * g_ref[...]
                        gr = gr_ref[rs, vs]
                        o_ref[rs, vs] = (y * (gr * jax.nn.sigmoid(gr))).astype(o_ref.dtype)
                    else:
                        of_scr[blk_rows(c), vs] = o
                s_scr[p] = s_new

    def blk_rows(c):
        pos = jnp.where(pss == 0, blk, nblk - 1 - blk)
        return pl.ds(pl.multiple_of(pos * GLA_BLOCK + c * C, C), C)

    @pl.when(pss == 0)
    def _fwd():
        run_direction(False)

    @pl.when(pss == 1)
    def _bwd():
        run_direction(True)

    @pl.when(blk == nblk - 1)
    def _fin():
        sfin_ref[0, 0] = s_scr[...]


def gla_mixer(pa, wd, bd, g, s0, *, row0, nseq, seqlen):
    nblk = seqlen // GLA_BLOCK
    b0 = row0 // GLA_BLOCK
    has_state = s0 is not None
    if s0 is None:
        s0 = jnp.zeros((2, nseq, 2, LANES, LANES), F32)

    def rb(s, p, b):
        return b0 + s * nblk + b + p * (nblk - 1 - 2 * b)

    def orb(s, p, b):
        return s * nblk + nblk - 1 - p * b

    kern = functools.partial(_gla_kernel, nblk=nblk, has_state=has_state)
    return pl.pallas_call(
        kern,
        grid=(nseq, 2, nblk),
        in_specs=[pl.BlockSpec((GLA_BLOCK, GLA_QK_W), lambda s, p, b: (rb(s, p, b), 0)),
                  pl.BlockSpec((GLA_BLOCK, GLA_QK_W), lambda s, p, b: (rb(s, p, b), 1)),
                  pl.BlockSpec((GLA_BLOCK, GLA_V_W), lambda s, p, b: (rb(s, p, b), 1)),
                  pl.BlockSpec((GLA_BLOCK, GLA_V_W), lambda s, p, b: (rb(s, p, b), 2)),
                  pl.BlockSpec((GLA_BLOCK, LANES), lambda s, p, b: (rb(s, p, b), GLA_A_W // LANES - 1)),
                  pl.BlockSpec((1, LANES, GLA_QK_W), lambda s, p, b: (p, 0, 0)),
                  pl.BlockSpec((1, 1, GLA_QK_W), lambda s, p, b: (p, 0, 0)),
                  pl.BlockSpec((1, GLA_DV), lambda s, p, b: (0, 0)),
                  pl.BlockSpec((1, 1, 2, LANES, LANES), lambda s, p, b: (p, s, 0, 0, 0))],
        out_specs=[pl.BlockSpec((GLA_BLOCK, GLA_V_W), lambda s, p, b: (orb(s, p, b), 0)),
                   pl.BlockSpec((1, 1, 2, LANES, LANES), lambda s, p, b: (p, s, 0, 0, 0))],
        out_shape=[jax.ShapeDtypeStruct((nseq * seqlen, GLA_V_W), BF16),
                   jax.ShapeDtypeStruct((2, nseq, 2, LANES, LANES), F32)],
        scratch_shapes=[pltpu.VMEM((2, LANES, LANES), F32),
                        pltpu.VMEM((seqlen, GLA_V_W), F32)],
        compiler_params=_params(("arbitrary", "arbitrary", "arbitrary")),
        name="gla_mixer",
    )(pa, pa, pa, pa, pa, wd, bd, g, s0)


def _rope_kernel(x_ref, cos_ref, sin_ref, o_ref):
    cos = cos_ref[...]
    sin = sin_ref[...]
    lane = lax.broadcasted_iota(jnp.int32, (1, LANES), 1)
    first = (lane % (2 * ROPE_NFREQ)) < ROPE_NFREQ
    nq = DIFF_QK_W // LANES
    for j in range(2 * nq):
        x = x_ref[:, j * LANES:(j + 1) * LANES].astype(F32)
        partner = jnp.where(first, pltpu.roll(x, LANES - ROPE_NFREQ, 1), pltpu.roll(x, ROPE_NFREQ, 1))
        y = x * cos + partner * sin
        if j < nq:
            y = y * DIFF_QSCALE
        o_ref[:, j * LANES:(j + 1) * LANES] = y.astype(o_ref.dtype)


def rope_qk(pb_dec, cos_t, sin_t, tm=512):
    nt = DEC_SEQ // tm
    return pl.pallas_call(
        _rope_kernel,
        grid=(M_DEC // tm,),
        in_specs=[pl.BlockSpec((tm, 2 * DIFF_QK_W), lambda i: (i, 0)),
                  pl.BlockSpec((tm, LANES), lambda i: (i % nt, 0)),
                  pl.BlockSpec((tm, LANES), lambda i: (i % nt, 0))],
        out_specs=pl.BlockSpec((tm, 2 * DIFF_QK_W), lambda i: (i, 0)),
        out_shape=jax.ShapeDtypeStruct((M_DEC, 2 * DIFF_QK_W), BF16),
        compiler_params=_params(("arbitrary",)),
        name="rope_qk",
    )(pb_dec, cos_t, sin_t)


def _diff_kernel(lam_ref, q_ref, kn_ref, vn_ref, *rest, has_cache, q_scale, lam_init):
    if has_cache:
        kc_ref, vc_ref, g_ref, o_ref, vaug, kc_scr = rest
    else:
        g_ref, o_ref, vaug = rest
    n_new = kn_ref.shape[0]
    n_c = PAST_LEN if has_cache else 0
    qb = pl.program_id(2)

    @pl.when(qb == 0)
    def _stage():
        vaug[:, LANES:] = jnp.ones((n_c + n_new, LANES), BF16)
        vaug[n_c:, :LANES] = vn_ref[...].astype(BF16)
        if has_cache:
            vaug[:n_c, :LANES] = vc_ref[0].astype(BF16)
            kc_scr[...] = kc_ref[0].astype(BF16)

    q = q_ref[...]
    if q_scale != 1.0:
        q = q * q_scale
    q = q.astype(BF16)
    lane = lax.broadcasted_iota(jnp.int32, (1, LANES), 1)
    qm = [jnp.where(lane < DIFF_DK, q, jnp.zeros_like(q)), jnp.where(lane >= DIFF_DK, q, jnp.zeros_like(q))]
    kchunk = min(DIFF_KCHUNK, n_new)
    m = [None, None]
    acc = [None, None]
    for c in range(-1 if has_cache else 0, n_new // kchunk):
        if c < 0:
            k = kc_scr[...]
            v = vaug[:n_c, :]
        else:
            k = kn_ref[c * kchunk:(c + 1) * kchunk, :].astype(BF16)
            v = vaug[n_c + c * kchunk:n_c + (c + 1) * kchunk, :]
        for mp in range(2):
            s = _dot_nt(qm[mp], k)
            cm = jnp.max(s, axis=-1, keepdims=True)
            if m[mp] is None:
                m[mp] = cm
                acc[mp] = _dot(jnp.exp2(s - cm).astype(BF16), v)
            else:
                m_new = jnp.maximum(m[mp], cm)
                acc[mp] = jnp.exp2(m[mp] - m_new) * acc[mp] + _dot(jnp.exp2(s - m_new).astype(BF16), v)
                m[mp] = m_new
    outs = [a[:, :LANES] / a[:, LANES:] for a in acc]
    o = outs[0] - lam_ref[0] * outs[1]
    y = o * lax.rsqrt(jnp.mean(o * o, axis=-1, keepdims=True) + NORM_EPS) * g_ref[...]
    o_ref[...] = (y * (1.0 - lam_init)).astype(o_ref.dtype)


def diff_attention_ctx(pb_ctx, lam, g, lam_init):
    kern = functools.partial(_diff_kernel, has_cache=False, q_scale=DIFF_QSCALE, lam_init=lam_init)
    H = DIFF_HEADS
    return pl.pallas_call(
        kern,
        grid=(BATCH, H, 1),
        in_specs=[pl.BlockSpec(memory_space=pltpu.SMEM),
                  pl.BlockSpec((SEQ, LANES), lambda s, h, b: (s, h)),
                  pl.BlockSpec((SEQ, LANES), lambda s, h, b: (s, H + h)),
                  pl.BlockSpec((SEQ, LANES), lambda s, h, b: (s, 2 * H + h)),
                  pl.BlockSpec((1, DIFF_DV), lambda s, h, b: (0, 0))],
        out_specs=pl.BlockSpec((SEQ, LANES), lambda s, h, b: (s, h)),
        out_shape=jax.ShapeDtypeStruct((M_CTX, DIFF_V_W), BF16),
        scratch_shapes=[pltpu.VMEM((SEQ, 2 * LANES), BF16)],
        compiler_params=_params(("arbitrary", "arbitrary", "arbitrary")),
        name="diff_attention_ctx",
    )(lam, pb_ctx, pb_ctx, pb_ctx, g)


def diff_attention_dec(qk_rope, pb_dec, cache_k, cache_v, lam, g, lam_init, tq=512):
    kern = functools.partial(_diff_kernel, has_cache=True, q_scale=1.0, lam_init=lam_init)
    H = DIFF_HEADS
    nqb = DEC_SEQ // tq
    ntot = PAST_LEN + DEC_SEQ
    return pl.pallas_call(
        kern,
        grid=(DEC_BATCH, H, nqb),
        in_specs=[pl.BlockSpec(memory_space=pltpu.SMEM),
                  pl.BlockSpec((tq, LANES), lambda s, h, b: (s * nqb + b, h)),
                  pl.BlockSpec((DEC_SEQ, LANES), lambda s, h, b: (s, H + h)),
                  pl.BlockSpec((DEC_SEQ, LANES), lambda s, h, b: (s, 2 * H + h)),
                  pl.BlockSpec((1, PAST_LEN, LANES), lambda s, h, b: (s, 0, h)),
                  pl.BlockSpec((1, PAST_LEN, LANES), lambda s, h, b: (s, 0, h)),
                  pl.BlockSpec((1, DIFF_DV), lambda s, h, b: (0, 0))],
        out_specs=pl.BlockSpec((tq, LANES), lambda s, h, b: (s * nqb + b, h)),
        out_shape=jax.ShapeDtypeStruct((M_DEC, DIFF_V_W), BF16),
        scratch_shapes=[pltpu.VMEM((ntot, 2 * LANES), BF16),
                        pltpu.VMEM((PAST_LEN, LANES), BF16)],
        compiler_params=_params(("arbitrary", "arbitrary", "arbitrary")),
        name="diff_attention_dec",
    )(lam, qk_rope, qk_rope, pb_dec, cache_k, cache_v, g)


def _softmax_pair(q, keys_vals, biases):
    lane = lax.broadcasted_iota(jnp.int32, (1, LANES), 1)
    outs = []
    for h in range(2):
        msk = (lane < NA_DK) if h == 0 else (lane >= NA_DK)
        qm = jnp.where(msk, q, jnp.zeros_like(q))
        ss = []
        for (k, _), bias in zip(keys_vals, biases):
            s = _dot_nt(qm, k)
            if bias is not None:
                s = s + bias(h)
            ss.append(s)
        m = functools.reduce(jnp.maximum, [jnp.max(s, axis=-1, keepdims=True) for s in ss])
        num = 0.0
        den = 0.0
        for s, (_, v) in zip(ss, keys_vals):
            e = jnp.exp(s - m)
            den = den + jnp.sum(e, axis=-1, keepdims=True)
            num = num + _dot(e.astype(BF16), v)
        outs.append(num / den)
    return jnp.where(lane < NA_DK, outs[0], outs[1])


def _ctx_attn_kernel(q_ref, k_ref, v_ref, o_ref):
    q = (q_ref[...] * (NA_DK ** -0.5)).astype(BF16)
    k = k_ref[...].astype(BF16)
    v = v_ref[...].astype(BF16)
    o_ref[...] = _softmax_pair(q, [(k, v)], [None]).astype(o_ref.dtype)


def context_attention_ctx(pc_ctx):
    hp = NA_HEADS // 2
    return pl.pallas_call(
        _ctx_attn_kernel,
        grid=(BATCH, hp),
        in_specs=[pl.BlockSpec((SEQ, LANES), lambda s, p: (s, p)),
                  pl.BlockSpec((SEQ, LANES), lambda s, p: (s, hp + p)),
                  pl.BlockSpec((SEQ, LANES), lambda s, p: (s, 2 * hp + p))],
        out_specs=pl.BlockSpec((SEQ, LANES), lambda s, p: (s, p)),
        out_shape=jax.ShapeDtypeStruct((M_CTX, NA_W), BF16),
        compiler_params=_params(("arbitrary", "arbitrary")),
        name="context_attention",
    )(pc_ctx, pc_ctx, pc_ctx)


def _na_window_start(rb):
    return jnp.clip(NA_QROWS * rb - NA_WIN_H // 2, 0, GRID_H - NA_KROWS)


def _na_kernel(q_ref, k_ref, v_ref, kc_ref, vc_ref, bias_ref, o_ref):
    rb = pl.program_id(2)
    start = pl.multiple_of(_na_window_start(rb) * GRID_W, GRID_W)
    kl = k_ref[pl.ds(start, NA_NLOC), :]
    vl = v_ref[pl.ds(start, NA_NLOC), :]
    kc = kc_ref[0].astype(BF16)
    vc = vc_ref[0].astype(BF16)
    q = q_ref[...] * (NA_DK ** -0.5)
    o = _softmax_pair(q, [(kl, vl), (kc, vc)], [lambda h: bias_ref[0, h], None])
    o_ref[...] = o.astype(o_ref.dtype)


def _na_situation(rb):
    nrb = GRID_H // NA_QROWS
    return jnp.where(rb < 2, rb, jnp.where(rb >= nrb - 2, rb - (nrb - 5), 2))


def neighbourhood_attention_dec(pc_dec, cache_k, cache_v, bias):
    hp = NA_HEADS // 2
    nrb = GRID_H // NA_QROWS
    return pl.pallas_call(
        _na_kernel,
        grid=(DEC_BATCH, hp, nrb),
        in_specs=[pl.BlockSpec((NA_QBLK, LANES), lambda s, p, r: (s * nrb + r, p)),
                  pl.BlockSpec((DEC_SEQ, LANES), lambda s, p, r: (s, hp + p)),
                  pl.BlockSpec((DEC_SEQ, LANES), lambda s, p, r: (s, 2 * hp + p)),
                  pl.BlockSpec((1, PAST_LEN, LANES), lambda s, p, r: (s, 0, p)),
                  pl.BlockSpec((1, PAST_LEN, LANES), lambda s, p, r: (s, 0, p)),
                  pl.BlockSpec((1, 2, NA_QBLK, NA_NLOC), lambda s, p, r: (_na_situation(r), p, 0, 0))],
        out_specs=pl.BlockSpec((NA_QBLK, LANES), lambda s, p, r: (s * nrb + r, p)),
        out_shape=jax.ShapeDtypeStruct((M_DEC, NA_W), BF16),
        compiler_params=_params(("arbitrary", "arbitrary", "arbitrary")),
        name="neighbourhood_attention",
    )(pc_dec, pc_dec, pc_dec, cache_k, cache_v, bias)


def _na_bias_tables(rpb):
    nrb = GRID_H // NA_QROWS
    rbs = [0, 1, 2, nrb - 2, nrb - 1]
    qc = np.arange(GRID_W)
    kc = np.arange(GRID_W)
    col_start = np.clip(qc - NA_WIN_W // 2, 0, GRID_W - NA_WIN_W)
    col_ok = (kc[None, :] >= col_start[:, None]) & (kc[None, :] < col_start[:, None] + NA_WIN_W)
    dcol = np.clip(kc[None, :] - qc[:, None], 1 - NA_WIN_W, NA_WIN_W - 1) + NA_WIN_W - 1
    drow_all, ok_all = [], []
    for rb in rbs:
        ws = int(np.clip(NA_QROWS * rb - NA_WIN_H // 2, 0, GRID_H - NA_KROWS))
        qr = NA_QROWS * rb + np.arange(NA_QROWS)
        kr = ws + np.arange(NA_KROWS)
        win = np.clip(qr - NA_WIN_H // 2, 0, GRID_H - NA_WIN_H)
        row_ok = (kr[None, :] >= win[:, None]) & (kr[None, :] < win[:, None] + NA_WIN_H)
        drow_all.append(np.clip(kr[None, :] - qr[:, None] + NA_WIN_H - 1, 0, 2 * NA_WIN_H - 2))
        ok_all.append(row_ok[:, None, :, None] & col_ok[None, :, None, :])
    nd, nc = 2 * NA_WIN_H - 1, 2 * NA_WIN_W - 1
    sel_row = (np.stack(drow_all).reshape(-1)[:, None] == np.arange(nd)[None, :]).astype(np.float32)
    sel_col = (np.arange(nc)[:, None] == dcol.reshape(-1)[None, :]).astype(np.float32)
    tab = jnp.einsum("nd,hdc,cx->hnx", sel_row, rpb.astype(F32), sel_col, precision=lax.Precision.HIGHEST)
    tab = tab.reshape(NA_HEADS, 5, NA_QROWS, NA_KROWS, GRID_W, GRID_W)
    tab = jnp.transpose(tab, (1, 0, 2, 4, 3, 5))
    ok = np.stack(ok_all)[:, None]
    return jnp.where(ok, tab, NEG_INF).reshape(5, NA_HEADS, NA_QBLK, NA_NLOC)


def _rope_tables():
    pos = np.arange(DEC_SEQ)
    row = (pos // GRID_W).astype(np.float32)
    col = (pos % GRID_W).astype(np.float32)
    inv = jnp.asarray(ROPE_BASE, F32) ** (-jnp.arange(ROPE_NFREQ, dtype=F32) / ROPE_NFREQ)
    ang_r = jnp.asarray(row)[:, None] * inv
    ang_c = jnp.asarray(col)[:, None] * inv
    cos64 = jnp.concatenate([jnp.cos(ang_r), jnp.cos(ang_r), jnp.cos(ang_c), jnp.cos(ang_c)], axis=1)
    sin64 = jnp.concatenate([-jnp.sin(ang_r), jnp.sin(ang_r), -jnp.sin(ang_c), jnp.sin(ang_c)], axis=1)
    return jnp.tile(cos64, (1, 2)), jnp.tile(sin64, (1, 2))


def kernel(x_prompt, x_sample, c, cache_diff_k, cache_diff_v, cache_na_k, cache_na_v, state_gla_fwd,
           state_gla_bwd, c_ctx, w_ada, b_ada, norm_mix_g, w_in, w_decay, b_decay, gla_norm_g, diff_lambda,
           diff_norm_g, na_rpb, w_br_a, w_br_b, w_br_c, w_out, norm_ffn_g, w_ffn_gate, w_ffn_up, w_ffn_down,
           norm_final_g):
    D = D_MODEL
    x = jnp.concatenate([x_prompt.reshape(M_CTX, D), x_sample.reshape(M_DEC, D)], axis=0)
    cond8 = jnp.concatenate([c_ctx[None], c, jnp.zeros((8 - N_COND, D), F32)], axis=0)
    mod = modulation_all(cond8, w_ada, b_ada)
    cos_t, sin_t = _rope_tables()

    offs = np.cumsum((0,) + IN_WIDTHS)
    a_end, b_end, c_end = offs[5], offs[8], offs[11]
    new_state = []
    for l in range(DEPTH):
        lam_init = 0.8 - 0.6 * float(np.exp(-0.3 * l))
        m6 = mod[l, :N_COND].reshape(N_COND, 6, 1, D)
        sh_a, sc_a, gt_a, sh_f, sc_f, gt_f = (m6[:, i] for i in range(6))
        wl = w_in[l]
        w_a = jnp.concatenate([wl[:, :a_end], jnp.zeros((D, GLA_A_W - a_end), F32)], axis=1).astype(BF16)
        w_b = wl[:, a_end:b_end].astype(BF16)
        w_c = wl[:, b_end:c_end].astype(BF16)
        w_g = wl[:, c_end:].astype(BF16)

        h = norm_mod(x, norm_mix_g[l][None], sc_a, sh_a)
        pa = matmul(h, w_a, F32, tm=512, tn=GLA_A_W, name="in_proj_gla")
        pb_ctx = matmul(h, w_b, F32, rows=M_CTX, name="in_proj_diff_ctx")
        pb_dec = matmul(h, w_b, BF16, row0=M_CTX, rows=M_DEC, name="in_proj_diff_dec")
        pc_ctx = matmul(h, w_c, F32, rows=M_CTX, tn=768, name="in_proj_na_ctx")
        pc_dec = matmul(h, w_c, BF16, row0=M_CTX, rows=M_DEC, tn=768, name="in_proj_na_dec")
        gates = matmul(h, w_g, BF16, act="sigmoid", name="in_proj_gates")

        wd = jnp.zeros((2, LANES, GLA_QK_W), F32)
        wd = wd.at[0, :GLA_LR].set(w_decay[l, 0]).at[1, GLA_LR:2 * GLA_LR].set(w_decay[l, 1])
        bd = b_decay[l][:, None, :]
        gn = gla_norm_g[l][None]
        s0 = jnp.stack([state_gla_fwd[:, l], state_gla_bwd[:, l]]).reshape(2, DEC_BATCH, 2, LANES, LANES)
        oa_ctx, s_fin = gla_mixer(pa, wd, bd, gn, None, row0=0, nseq=BATCH, seqlen=SEQ)
        oa_dec, _ = gla_mixer(pa, wd, bd, gn, s0, row0=M_CTX, nseq=DEC_BATCH, seqlen=DEC_SEQ)
        oa = jnp.concatenate([oa_ctx, oa_dec], axis=0)

        lp = diff_lambda[l].astype(F32)
        lam = (jnp.exp(jnp.sum(lp[0] * lp[1])) - jnp.exp(jnp.sum(lp[2] * lp[3])) + lam_init).reshape(1)
        dg = diff_norm_g[l][None]
        ob_ctx = diff_attention_ctx(pb_ctx, lam, dg, lam_init)
        qk_rope = rope_qk(pb_dec, cos_t, sin_t)
        ob_dec = diff_attention_dec(qk_rope, pb_dec,
                                    cache_diff_k[:, l].reshape(DEC_BATCH, PAST_LEN, DIFF_QK_W),
                                    cache_diff_v[:, l].reshape(DEC_BATCH, PAST_LEN, DIFF_V_W),
                                    lam, dg, lam_init)
        ob = jnp.concatenate([ob_ctx, ob_dec], axis=0)

        oc_ctx = context_attention_ctx(pc_ctx)
        oc_dec = neighbourhood_attention_dec(pc_dec,
                                             cache_na_k[:, l].reshape(DEC_BATCH, PAST_LEN, NA_W),
                                             cache_na_v[:, l].reshape(DEC_BATCH, PAST_LEN, NA_W),
                                             _na_bias_tables(na_rpb[l]))
        oc = jnp.concatenate([oc_ctx, oc_dec], axis=0)

        merged = branch_merge(oa, ob, oc, gates, w_br_a[l].astype(BF16), w_br_b[l].astype(BF16),
                              w_br_c[l].astype(BF16))
        x = resid_matmul(merged, w_out[l].astype(BF16), x, gt_a, tm=1024, tn=1024, name="out_proj")

        h2 = norm_mod(x, norm_ffn_g[l][None], sc_f, sh_f)
        act = ffn_up(h2, w_ffn_gate[l].astype(BF16), w_ffn_up[l].astype(BF16))
        x = resid_matmul(act, w_ffn_down[l].astype(BF16), x, gt_f, tm=512, tn=512, name="ffn_down")

        new_state.append((
            pb_ctx[:, DIFF_QK_W:2 * DIFF_QK_W].reshape(BATCH, SEQ, DIFF_HEADS, 2 * DIFF_DK),
            pb_ctx[:, 2 * DIFF_QK_W:].reshape(BATCH, SEQ, DIFF_HEADS, DIFF_DV),
            pc_ctx[:, NA_W:2 * NA_W].reshape(BATCH, SEQ, NA_HEADS, NA_DK),
            pc_ctx[:, 2 * NA_W:].reshape(BATCH, SEQ, NA_HEADS, NA_DK),
            s_fin[0].reshape(BATCH, GLA_HEADS, GLA_DK, GLA_DV),
            s_fin[1].reshape(BATCH, GLA_HEADS, GLA_DK, GLA_DV)))

    gfin = norm_final_g[None]
    y_prompt = final_norm(x, gfin, 0, M_CTX).reshape(BATCH, SEQ, D)
    y_sample = final_norm(x, gfin, M_CTX, M_DEC).reshape(DEC_BATCH, DEC_SEQ, D)
    stacked = tuple(jnp.stack([st[i] for st in new_state], axis=1) for i in range(6))
    return (y_prompt, y_sample) + stacked
```

```python
import functools

import numpy as np
import jax
import jax.numpy as jnp
from jax import lax
from jax.experimental import pallas as pl
from jax.experimental.pallas import tpu as pltpu

F32 = jnp.float32
BF16 = jnp.bfloat16

D_MODEL = 2048
BATCH = 16
SEQ = 256
DEPTH = 2
DEC_BATCH = 2
DEC_SEQ = 4096
PAST_LEN = 512
GRID_W = 64
GRID_H = DEC_SEQ // GRID_W
NORM_EPS = 1e-6
NEG_INF = -1e30
GLA_HEADS = 4
GLA_DK = 64
GLA_DV = 128
GLA_LR = 16
GLA_TAU = 16.0
GLA_CHUNK = 64
GLA_QK_W = GLA_HEADS * GLA_DK
GLA_V_W = GLA_HEADS * GLA_DV
DIFF_HEADS = 8
DIFF_DK = 64
DIFF_DV = 128
DIFF_QK_W = DIFF_HEADS * 2 * DIFF_DK
DIFF_V_W = DIFF_HEADS * DIFF_DV
ROPE_BASE = 10000.0
ROPE_NFREQ = DIFF_DK // 4
NA_HEADS = 8
NA_DK = 64
NA_W = NA_HEADS * NA_DK
NA_WIN_H = 8
NA_WIN_W = 16
D_FF = ((8 * D_MODEL // 3 + 255) // 256) * 256
IN_WIDTHS = (GLA_QK_W, GLA_QK_W, GLA_V_W, GLA_V_W, 2 * GLA_LR,
             DIFF_QK_W, DIFF_QK_W, DIFF_V_W, NA_W, NA_W, NA_W, 3 * D_MODEL)

M_CTX = BATCH * SEQ
M_DEC = DEC_BATCH * DEC_SEQ
M_ALL = M_CTX + M_DEC
N_COND = 1 + DEC_BATCH

LANES = 128
VMEM_LIMIT_BYTES = 56 * 1024 * 1024

GLA_A_W = 13 * LANES
GLA_BLOCK = 256
NA_QROWS = 2
NA_QBLK = NA_QROWS * GRID_W
NA_KROWS = NA_WIN_H + NA_QROWS - 1
NA_NLOC = NA_KROWS * GRID_W
DIFF_KCHUNK = 512
LOG2E = 1.4426950408889634
DIFF_QSCALE = DIFF_DK ** -0.5 * LOG2E


def _params(sem):
    return pltpu.CompilerParams(dimension_semantics=sem, vmem_limit_bytes=VMEM_LIMIT_BYTES)


def _cond_of_row(row):
    return jnp.where(row < M_CTX, 0, 1 + (row - M_CTX) // DEC_SEQ)


def _dot(a, b):
    return jnp.dot(a, b, preferred_element_type=F32)


def _dot_nt(a, b):
    return lax.dot_general(a, b, (((1,), (1,)), ((), ())), preferred_element_type=F32)


def _dot_tn(a, b):
    return lax.dot_general(a, b, (((0,), (0,)), ((), ())), preferred_element_type=F32)


def _split3(x):
    hi = x.astype(BF16)
    r1 = x - hi.astype(F32)
    mid = r1.astype(BF16)
    lo = (r1 - mid.astype(F32)).astype(BF16)
    return hi, mid, lo


def _mod_kernel(c_ref, w_ref, b_ref, o_ref):
    c = c_ref[...]
    a = (c * jax.nn.sigmoid(c)).astype(BF16)
    o_ref[0] = _dot(a, w_ref[0].astype(BF16)) + b_ref[0]


def modulation_all(cond8, w_ada, b_ada):
    n = w_ada.shape[-1]
    tn = 1024
    return pl.pallas_call(
        _mod_kernel,
        grid=(DEPTH, n // tn),
        in_specs=[pl.BlockSpec((8, D_MODEL), lambda l, j: (0, 0)),
                  pl.BlockSpec((1, D_MODEL, tn), lambda l, j: (l, 0, j)),
                  pl.BlockSpec((1, 1, tn), lambda l, j: (l, 0, j))],
        out_specs=pl.BlockSpec((1, 8, tn), lambda l, j: (l, 0, j)),
        out_shape=jax.ShapeDtypeStruct((DEPTH, 8, n), F32),
        compiler_params=_params(("arbitrary", "arbitrary")),
        name="modulation",
    )(cond8, w_ada, b_ada.reshape(DEPTH, 1, n))


def _norm_mod_kernel(xa_ref, xb_ref, g_ref, sc_ref, sh_ref, o_ref, *, n_a):
    def body(x_ref):
        x = x_ref[...]
        y = x * lax.rsqrt(jnp.mean(x * x, axis=-1, keepdims=True) + NORM_EPS) * g_ref[0]
        o_ref[...] = (y * (1.0 + sc_ref[0]) + sh_ref[0]).astype(o_ref.dtype)

    i = pl.program_id(0)

    @pl.when(i < n_a)
    def _():
        body(xa_ref)

    @pl.when(i >= n_a)
    def _():
        body(xb_ref)


def norm_mod(xs, g, layer, sc, sh, tm=512):
    n_a = M_CTX // tm
    amap, bmap = _two_source_maps(n_a, len(xs) == 1)
    xa, xb = (xs[0], xs[0]) if len(xs) == 1 else xs
    cmap = lambda i: (_cond_of_row(i * tm), 0, 0)
    return pl.pallas_call(
        functools.partial(_norm_mod_kernel, n_a=n_a),
        grid=(M_ALL // tm,),
        in_specs=[pl.BlockSpec((tm, D_MODEL), lambda i: (amap(i), 0)),
                  pl.BlockSpec((tm, D_MODEL), lambda i: (bmap(i), 0)),
                  pl.BlockSpec((1, 1, D_MODEL), lambda i: (layer, 0, 0)),
                  pl.BlockSpec((1, 1, D_MODEL), cmap),
                  pl.BlockSpec((1, 1, D_MODEL), cmap)],
        out_specs=pl.BlockSpec((tm, D_MODEL), lambda i: (i, 0)),
        out_shape=jax.ShapeDtypeStruct((M_ALL, D_MODEL), BF16),
        compiler_params=_params(("arbitrary",)),
        name="norm_mod",
    )(xa, xb, g, sc, sh)


def _rmsnorm_kernel(x_ref, g_ref, o_ref):
    x = x_ref[...]
    o_ref[...] = x * lax.rsqrt(jnp.mean(x * x, axis=-1, keepdims=True) + NORM_EPS) * g_ref[...]


def final_norm(x, g, row0, rows, tm=512):
    return pl.pallas_call(
        _rmsnorm_kernel,
        grid=(rows // tm,),
        in_specs=[pl.BlockSpec((tm, D_MODEL), lambda i: (i + row0 // tm, 0)),
                  pl.BlockSpec((1, D_MODEL), lambda i: (0, 0))],
        out_specs=pl.BlockSpec((tm, D_MODEL), lambda i: (i, 0)),
        out_shape=jax.ShapeDtypeStruct((rows, D_MODEL), F32),
        compiler_params=_params(("arbitrary",)),
        name="final_norm",
    )(x, g)


def _mm_kernel(a_ref, w_ref, o_ref, *, act):
    acc = _dot(a_ref[...], w_ref[...])
    if act == "sigmoid":
        acc = jax.nn.sigmoid(acc)
    o_ref[...] = acc.astype(o_ref.dtype)


def matmul(a, w, out_dtype, *, row0=0, rows=None, tm=1024, tn=1024, act=None, name="matmul"):
    k = a.shape[1]
    n = w.shape[1]
    rows = a.shape[0] if rows is None else rows
    tn = min(tn, n)
    assert rows % tm == 0 and row0 % tm == 0 and n % tn == 0
    r0 = row0 // tm
    return pl.pallas_call(
        functools.partial(_mm_kernel, act=act),
        grid=(rows // tm, n // tn),
        in_specs=[pl.BlockSpec((tm, k), lambda i, j: (i + r0, 0)),
                  pl.BlockSpec((k, tn), lambda i, j: (0, j))],
        out_specs=pl.BlockSpec((tm, tn), lambda i, j: (i, j)),
        out_shape=jax.ShapeDtypeStruct((rows, n), out_dtype),
        compiler_params=_params(("arbitrary", "arbitrary")),
        name=name,
    )(a, w)


def _stage_weights(i, pairs):
    @pl.when(i == 0)
    def _():
        for w_ref, w_scr in pairs:
            w_scr[...] = w_ref[0].astype(BF16)


def _merge_kernel(oa_ref, ob_ref, oc_ref, ga_ref, gb_ref, gc_ref, wa_ref, wb_ref, wc_ref, o_ref,
                  wa_scr, wb_scr, wc_scr):
    _stage_weights(pl.program_id(1), ((wa_ref, wa_scr), (wb_ref, wb_scr), (wc_ref, wc_scr)))
    y = ga_ref[...].astype(F32) * _dot(oa_ref[...], wa_scr[...])
    y += gb_ref[...].astype(F32) * _dot(ob_ref[...], wb_scr[...])
    y += gc_ref[...].astype(F32) * _dot(oc_ref[...], wc_scr[...])
    o_ref[...] = y.astype(o_ref.dtype)


def branch_merge(oa, ob, oc, gates, wa, wb, wc, layer, tm=1024, tn=1024):
    m = oa.shape[0]
    nb = D_MODEL // tn
    row = lambda j, i: (i, 0)
    col = lambda j, i: (layer, 0, j)
    return pl.pallas_call(
        _merge_kernel,
        grid=(nb, m // tm),
        in_specs=[pl.BlockSpec((tm, GLA_V_W), row),
                  pl.BlockSpec((tm, DIFF_V_W), row),
                  pl.BlockSpec((tm, NA_W), row),
                  pl.BlockSpec((tm, tn), lambda j, i: (i, j)),
                  pl.BlockSpec((tm, tn), lambda j, i: (i, j + nb)),
                  pl.BlockSpec((tm, tn), lambda j, i: (i, j + 2 * nb)),
                  pl.BlockSpec((1, GLA_V_W, tn), col),
                  pl.BlockSpec((1, DIFF_V_W, tn), col),
                  pl.BlockSpec((1, NA_W, tn), col)],
        out_specs=pl.BlockSpec((tm, tn), lambda j, i: (i, j)),
        out_shape=jax.ShapeDtypeStruct((m, D_MODEL), BF16),
        scratch_shapes=[pltpu.VMEM((GLA_V_W, tn), BF16), pltpu.VMEM((DIFF_V_W, tn), BF16),
                        pltpu.VMEM((NA_W, tn), BF16)],
        compiler_params=_params(("arbitrary", "arbitrary")),
        name="branch_merge",
    )(oa, ob, oc, gates, gates, gates, wa, wb, wc)


def _two_source_maps(n_a, stacked):
    b_off = n_a if stacked else 0
    return (lambda i: jnp.minimum(i, n_a - 1)), (lambda i: jnp.maximum(i - n_a, 0) + b_off)


def _resid_mm_kernel(a_ref, w_ref, xa_ref, xb_ref, gt_ref, o_ref, w_scr, *, n_a):
    i = pl.program_id(1)
    _stage_weights(i, ((w_ref, w_scr),))
    y = gt_ref[0] * _dot(a_ref[...], w_scr[...])

    @pl.when(i < n_a)
    def _():
        o_ref[...] = xa_ref[...] + y

    @pl.when(i >= n_a)
    def _():
        o_ref[...] = xb_ref[...] + y


def resid_matmul(a, w, xs, gt, layer, tm, tn, name):
    m, k = a.shape
    n = w.shape[2]
    n_a = M_CTX // tm
    amap, bmap = _two_source_maps(n_a, len(xs) == 1)
    xa, xb = (xs[0], xs[0]) if len(xs) == 1 else xs
    return pl.pallas_call(
        functools.partial(_resid_mm_kernel, n_a=n_a),
        grid=(n // tn, m // tm),
        in_specs=[pl.BlockSpec((tm, k), lambda j, i: (i, 0)),
                  pl.BlockSpec((1, k, tn), lambda j, i: (layer, 0, j)),
                  pl.BlockSpec((tm, tn), lambda j, i: (amap(i), j)),
                  pl.BlockSpec((tm, tn), lambda j, i: (bmap(i), j)),
                  pl.BlockSpec((1, 1, tn), lambda j, i: (_cond_of_row(i * tm), 0, j))],
        out_specs=pl.BlockSpec((tm, tn), lambda j, i: (i, j)),
        out_shape=jax.ShapeDtypeStruct((m, n), F32),
        scratch_shapes=[pltpu.VMEM((k, tn), BF16)],
        compiler_params=_params(("arbitrary", "arbitrary")),
        name=name,
    )(a, w, xa, xb, gt)


def _ffn_up_kernel(h_ref, wg_ref, wu_ref, o_ref, wg_scr, wu_scr):
    _stage_weights(pl.program_id(1), ((wg_ref, wg_scr), (wu_ref, wu_scr)))
    h = h_ref[...]
    g = _dot(h, wg_scr[...])
    u = _dot(h, wu_scr[...])
    o_ref[...] = (g * jax.nn.sigmoid(g) * u).astype(o_ref.dtype)


def ffn_up(h, wg, wu, layer, tm=1024, tn=512):
    m = h.shape[0]
    wspec = pl.BlockSpec((1, D_MODEL, tn), lambda j, i: (layer, 0, j))
    return pl.pallas_call(
        _ffn_up_kernel,
        grid=(D_FF // tn, m // tm),
        in_specs=[pl.BlockSpec((tm, D_MODEL), lambda j, i: (i, 0)), wspec, wspec],
        out_specs=pl.BlockSpec((tm, tn), lambda j, i: (i, j)),
        out_shape=jax.ShapeDtypeStruct((m, D_FF), BF16),
        scratch_shapes=[pltpu.VMEM((D_MODEL, tn), BF16), pltpu.VMEM((D_MODEL, tn), BF16)],
        compiler_params=_params(("arbitrary", "arbitrary")),
        name="ffn_up",
    )(h, wg, wu)


def _gla_kernel(q_ref, k_ref, v_ref, gr_ref, lr_ref, wd_ref, bd_ref, g_ref, *rest, nblk, has_state):
    if has_state:
        s0f_ref, s0b_ref, _, o_ref, sfin_ref, s_scr, of_scr = rest
    else:
        o_ref, sfin_ref, s_scr, of_scr = rest
    pss = pl.program_id(1)
    blk = pl.program_id(2)
    C = GLA_CHUNK
    nch = GLA_BLOCK // C

    if has_state:
        @pl.when(jnp.logical_and(blk == 0, pss == 0))
        def _init_f():
            s_scr[...] = s0f_ref[0, 0]

        @pl.when(jnp.logical_and(blk == 0, pss == 1))
        def _init_b():
            s_scr[...] = s0b_ref[0, 0]
    else:
        @pl.when(blk == 0)
        def _init():
            s_scr[...] = jnp.zeros_like(s_scr)

    lr = lr_ref[...]
    lr_hi = lr.astype(BF16)
    lr_lo = (lr - lr_hi.astype(F32)).astype(BF16)
    wd = wd_ref[0]
    wd_hi = wd.astype(BF16)
    wd_lo = (wd - wd_hi.astype(F32)).astype(BF16)
    pre = _dot(lr_hi, wd_hi) + _dot(lr_hi, wd_lo) + _dot(lr_lo, wd_hi) + bd_ref[0]
    la_all = jax.nn.log_sigmoid(pre) * (1.0 / GLA_TAU)

    row = lax.broadcasted_iota(jnp.int32, (C, C), 0)
    colm = lax.broadcasted_iota(jnp.int32, (C, C), 1)
    lane = lax.broadcasted_iota(jnp.int32, (1, LANES), 1)
    head_mask = (lane < GLA_DK, lane >= GLA_DK)
    eye = (lax.broadcasted_iota(jnp.int32, (LANES, LANES), 0)
           == lax.broadcasted_iota(jnp.int32, (LANES, LANES), 1))

    def run_direction(backward):
        keep = (colm >= row) if backward else (colm <= row)
        tri = jnp.where(keep, 1.0, 0.0).astype(BF16)
        last, mid = (0, C // 2) if backward else (C - 1, C // 2 - 1)
        order = range(nch - 1, -1, -1) if backward else range(nch)
        for c in order:
            rs = slice(c * C, (c + 1) * C)
            for p in range(GLA_HEADS // 2):
                ls = slice(p * LANES, (p + 1) * LANES)
                la = la_all[rs, ls]
                hi, md, lo = _split3(la)
                b = _dot(tri, hi) + _dot(tri, md) + _dot(tri, lo)
                b_last = b[last:last + 1, :]
                b_mid = b[mid:mid + 1, :]
                q = q_ref[rs, ls] * (GLA_DK ** -0.5)
                k = k_ref[rs, ls]
                q_in = q * jnp.exp(b - b_mid)
                k_in = (k * jnp.exp(b_mid - b)).astype(BF16)
                q_ex = q * jnp.exp(b)
                k_dc = k * jnp.exp(b_last - b)
                dec_col = jnp.exp(jnp.sum(jnp.where(eye, b_last, 0.0), axis=1, keepdims=True))
                s_old = s_scr[p]
                s_old_bf = s_old.astype(BF16)
                s_new = dec_col * s_old
                for h in range(2):
                    hd = 2 * p + h
                    vs = slice(hd * GLA_DV, (hd + 1) * GLA_DV)
                    v = v_ref[rs, vs].astype(BF16)
                    att = _dot_nt(jnp.where(head_mask[h], q_in, 0.0).astype(BF16), k_in)
                    att = jnp.where(keep, att, 0.0).astype(BF16)
                    o = _dot(att, v) + _dot(jnp.where(head_mask[h], q_ex, 0.0).astype(BF16), s_old_bf)
                    s_new = s_new + _dot_tn(jnp.where(head_mask[h], k_dc, 0.0).astype(BF16), v)
                    if backward:
                        tot = of_scr[blk_rows(c), vs] + o
                        y = tot * lax.rsqrt(jnp.mean(tot * tot, axis=-1, keepdims=True) + NORM_EPS) * g_ref[0]
                        gr = gr_ref[rs, vs]
                        o_ref[rs, vs] = (y * (gr * jax.nn.sigmoid(gr))).astype(o_ref.dtype)
                    else:
                        of_scr[blk_rows(c), vs] = o
                s_scr[p] = s_new

    def blk_rows(c):
        pos = jnp.where(pss == 0, blk, nblk - 1 - blk)
        return pl.ds(pl.multiple_of(pos * GLA_BLOCK + c * C, C), C)

    @pl.when(pss == 0)
    def _fwd():
        run_direction(False)

    @pl.when(pss == 1)
    def _bwd():
        run_direction(True)

    @pl.when(blk == nblk - 1)
    def _fin():
        sfin_ref[0, 0] = s_scr[...]


def gla_mixer(pa, wd, bd, g, layer, *, row0, nseq, seqlen, state=None, into=None):
    nblk = seqlen // GLA_BLOCK
    b0 = row0 // GLA_BLOCK
    has_state = state is not None
    assert has_state == (into is not None)

    def rb(s, p, b):
        return b0 + s * nblk + b + p * (nblk - 1 - 2 * b)

    def orb(s, p, b):
        return b0 + s * nblk + nblk - 1 - p * b

    in_specs = [pl.BlockSpec((GLA_BLOCK, GLA_QK_W), lambda s, p, b: (rb(s, p, b), 0)),
                pl.BlockSpec((GLA_BLOCK, GLA_QK_W), lambda s, p, b: (rb(s, p, b), 1)),
                pl.BlockSpec((GLA_BLOCK, GLA_V_W), lambda s, p, b: (rb(s, p, b), 1)),
                pl.BlockSpec((GLA_BLOCK, GLA_V_W), lambda s, p, b: (rb(s, p, b), 2)),
                pl.BlockSpec((GLA_BLOCK, LANES), lambda s, p, b: (rb(s, p, b), GLA_A_W // LANES - 1)),
                pl.BlockSpec((1, LANES, GLA_QK_W), lambda s, p, b: (p, 0, 0)),
                pl.BlockSpec((1, 1, GLA_QK_W), lambda s, p, b: (p, 0, 0)),
                pl.BlockSpec((1, 1, GLA_DV), lambda s, p, b: (layer, 0, 0))]
    args = [pa, pa, pa, pa, pa, wd, bd, g]
    aliases = {}
    if has_state:
        st_spec = pl.BlockSpec((1, 1, 2, LANES, LANES), lambda s, p, b: (s, layer, 0, 0, 0))
        in_specs += [st_spec, st_spec, pl.BlockSpec(memory_space=pl.ANY)]
        args += [state[0], state[1], into]
        aliases = {len(args) - 1: 0}
    kern = functools.partial(_gla_kernel, nblk=nblk, has_state=has_state)
    return pl.pallas_call(
        kern,
        grid=(nseq, 2, nblk),
        in_specs=in_specs,
        out_specs=[pl.BlockSpec((GLA_BLOCK, GLA_V_W), lambda s, p, b: (orb(s, p, b), 0)),
                   pl.BlockSpec((1, 1, 2, LANES, LANES), lambda s, p, b: (p, s, 0, 0, 0))],
        out_shape=[jax.ShapeDtypeStruct((M_ALL, GLA_V_W), BF16),
                   jax.ShapeDtypeStruct((2, nseq, 2, LANES, LANES), F32)],
        scratch_shapes=[pltpu.VMEM((2, LANES, LANES), F32),
                        pltpu.VMEM((seqlen, GLA_V_W), F32)],
        input_output_aliases=aliases,
        compiler_params=_params(("arbitrary", "arbitrary", "arbitrary")),
        name="gla_mixer",
    )(*args)


def _rope_kernel(x_ref, cos_ref, sin_ref, o_ref):
    cos = cos_ref[...]
    sin = sin_ref[...]
    lane = lax.broadcasted_iota(jnp.int32, (1, LANES), 1)
    first = (lane % (2 * ROPE_NFREQ)) < ROPE_NFREQ
    nq = DIFF_QK_W // LANES
    for j in range(2 * nq):
        x = x_ref[:, j * LANES:(j + 1) * LANES].astype(F32)
        partner = jnp.where(first, pltpu.roll(x, LANES - ROPE_NFREQ, 1), pltpu.roll(x, ROPE_NFREQ, 1))
        y = x * cos + partner * sin
        if j < nq:
            y = y * DIFF_QSCALE
        o_ref[:, j * LANES:(j + 1) * LANES] = y.astype(o_ref.dtype)


def rope_qk(pb_dec, cos_t, sin_t, tm=512):
    nt = DEC_SEQ // tm
    return pl.pallas_call(
        _rope_kernel,
        grid=(M_DEC // tm,),
        in_specs=[pl.BlockSpec((tm, 2 * DIFF_QK_W), lambda i: (i, 0)),
                  pl.BlockSpec((tm, LANES), lambda i: (i % nt, 0)),
                  pl.BlockSpec((tm, LANES), lambda i: (i % nt, 0))],
        out_specs=pl.BlockSpec((tm, 2 * DIFF_QK_W), lambda i: (i, 0)),
        out_shape=jax.ShapeDtypeStruct((M_DEC, 2 * DIFF_QK_W), BF16),
        compiler_params=_params(("arbitrary",)),
        name="rope_qk",
    )(pb_dec, cos_t, sin_t)


def _diff_kernel(lam_ref, q_ref, kn_ref, vn_ref, *rest, has_cache, q_scale, lam_init):
    if has_cache:
        kc_ref, vc_ref, g_ref, _, o_ref, vaug, kc_scr = rest
    else:
        g_ref, o_ref, vaug = rest
    n_new = kn_ref.shape[0]
    n_c = PAST_LEN if has_cache else 0
    qb = pl.program_id(2)

    @pl.when(qb == 0)
    def _stage():
        vaug[:, LANES:] = jnp.ones((n_c + n_new, LANES), BF16)
        vaug[n_c:, :LANES] = vn_ref[...].astype(BF16)
        if has_cache:
            vaug[:n_c, :LANES] = vc_ref[0, 0].astype(BF16)
            kc_scr[...] = kc_ref[0, 0].astype(BF16)

    q = q_ref[...]
    if q_scale != 1.0:
        q = q * q_scale
    q = q.astype(BF16)
    lane = lax.broadcasted_iota(jnp.int32, (1, LANES), 1)
    qm = [jnp.where(lane < DIFF_DK, q, jnp.zeros_like(q)), jnp.where(lane >= DIFF_DK, q, jnp.zeros_like(q))]
    kchunk = min(DIFF_KCHUNK, n_new)
    m = [None, None]
    acc = [None, None]
    for c in range(-1 if has_cache else 0, n_new // kchunk):
        if c < 0:
            k = kc_scr[...]
            v = vaug[:n_c, :]
        else:
            k = kn_ref[c * kchunk:(c + 1) * kchunk, :].astype(BF16)
            v = vaug[n_c + c * kchunk:n_c + (c + 1) * kchunk, :]
        for mp in range(2):
            s = _dot_nt(qm[mp], k)
            cm = jnp.max(s, axis=-1, keepdims=True)
            if m[mp] is None:
                m[mp] = cm
                acc[mp] = _dot(jnp.exp2(s - cm).astype(BF16), v)
            else:
                m_new = jnp.maximum(m[mp], cm)
                acc[mp] = jnp.exp2(m[mp] - m_new) * acc[mp] + _dot(jnp.exp2(s - m_new).astype(BF16), v)
                m[mp] = m_new
    outs = [a[:, :LANES] / a[:, LANES:] for a in acc]
    o = outs[0] - lam_ref[0] * outs[1]
    y = o * lax.rsqrt(jnp.mean(o * o, axis=-1, keepdims=True) + NORM_EPS) * g_ref[0]
    o_ref[...] = (y * (1.0 - lam_init)).astype(o_ref.dtype)


def diff_attention_ctx(pb_ctx, lam, g, layer, lam_init):
    kern = functools.partial(_diff_kernel, has_cache=False, q_scale=DIFF_QSCALE, lam_init=lam_init)
    H = DIFF_HEADS
    return pl.pallas_call(
        kern,
        grid=(BATCH, H, 1),
        in_specs=[pl.BlockSpec(memory_space=pltpu.SMEM),
                  pl.BlockSpec((SEQ, LANES), lambda s, h, b: (s, h)),
                  pl.BlockSpec((SEQ, LANES), lambda s, h, b: (s, H + h)),
                  pl.BlockSpec((SEQ, LANES), lambda s, h, b: (s, 2 * H + h)),
                  pl.BlockSpec((1, 1, DIFF_DV), lambda s, h, b: (layer, 0, 0))],
        out_specs=pl.BlockSpec((SEQ, LANES), lambda s, h, b: (s, h)),
        out_shape=jax.ShapeDtypeStruct((M_ALL, DIFF_V_W), BF16),
        scratch_shapes=[pltpu.VMEM((SEQ, 2 * LANES), BF16)],
        compiler_params=_params(("arbitrary", "arbitrary", "arbitrary")),
        name="diff_attention_ctx",
    )(lam, pb_ctx, pb_ctx, pb_ctx, g)


def diff_attention_dec(qk_rope, pb_dec, cache_k, cache_v, layer, lam, g, lam_init, into, tq=512):
    kern = functools.partial(_diff_kernel, has_cache=True, q_scale=1.0, lam_init=lam_init)
    H = DIFF_HEADS
    nqb = DEC_SEQ // tq
    ntot = PAST_LEN + DEC_SEQ
    ob0 = M_CTX // tq
    return pl.pallas_call(
        kern,
        grid=(DEC_BATCH, H, nqb),
        in_specs=[pl.BlockSpec(memory_space=pltpu.SMEM),
                  pl.BlockSpec((tq, LANES), lambda s, h, b: (s * nqb + b, h)),
                  pl.BlockSpec((DEC_SEQ, LANES), lambda s, h, b: (s, H + h)),
                  pl.BlockSpec((DEC_SEQ, LANES), lambda s, h, b: (s, 2 * H + h)),
                  pl.BlockSpec((1, 1, PAST_LEN, LANES), lambda s, h, b: (s, layer, 0, h)),
                  pl.BlockSpec((1, 1, PAST_LEN, LANES), lambda s, h, b: (s, layer, 0, h)),
                  pl.BlockSpec((1, 1, DIFF_DV), lambda s, h, b: (layer, 0, 0)),
                  pl.BlockSpec(memory_space=pl.ANY)],
        out_specs=pl.BlockSpec((tq, LANES), lambda s, h, b: (ob0 + s * nqb + b, h)),
        out_shape=jax.ShapeDtypeStruct((M_ALL, DIFF_V_W), BF16),
        scratch_shapes=[pltpu.VMEM((ntot, 2 * LANES), BF16),
                        pltpu.VMEM((PAST_LEN, LANES), BF16)],
        input_output_aliases={7: 0},
        compiler_params=_params(("arbitrary", "arbitrary", "arbitrary")),
        name="diff_attention_dec",
    )(lam, qk_rope, qk_rope, pb_dec, cache_k, cache_v, g, into)


def _softmax_pair(q, keys_vals, biases):
    lane = lax.broadcasted_iota(jnp.int32, (1, LANES), 1)
    outs = []
    for h in range(2):
        msk = (lane < NA_DK) if h == 0 else (lane >= NA_DK)
        qm = jnp.where(msk, q, jnp.zeros_like(q))
        ss = []
        for (k, _), bias in zip(keys_vals, biases):
            s = _dot_nt(qm, k)
            if bias is not None:
                s = s + bias(h)
            ss.append(s)
        m = functools.reduce(jnp.maximum, [jnp.max(s, axis=-1, keepdims=True) for s in ss])
        num = 0.0
        den = 0.0
        for s, (_, v) in zip(ss, keys_vals):
            e = jnp.exp(s - m)
            den = den + jnp.sum(e, axis=-1, keepdims=True)
            num = num + _dot(e.astype(BF16), v)
        outs.append(num / den)
    return jnp.where(lane < NA_DK, outs[0], outs[1])


def _ctx_attn_kernel(q_ref, k_ref, v_ref, o_ref):
    q = (q_ref[...] * (NA_DK ** -0.5)).astype(BF16)
    k = k_ref[...].astype(BF16)
    v = v_ref[...].astype(BF16)
    o_ref[...] = _softmax_pair(q, [(k, v)], [None]).astype(o_ref.dtype)


def context_attention_ctx(pc_ctx):
    hp = NA_HEADS // 2
    return pl.pallas_call(
        _ctx_attn_kernel,
        grid=(BATCH, hp),
        in_specs=[pl.BlockSpec((SEQ, LANES), lambda s, p: (s, p)),
                  pl.BlockSpec((SEQ, LANES), lambda s, p: (s, hp + p)),
                  pl.BlockSpec((SEQ, LANES), lambda s, p: (s, 2 * hp + p))],
        out_specs=pl.BlockSpec((SEQ, LANES), lambda s, p: (s, p)),
        out_shape=jax.ShapeDtypeStruct((M_ALL, NA_W), BF16),
        compiler_params=_params(("arbitrary", "arbitrary")),
        name="context_attention",
    )(pc_ctx, pc_ctx, pc_ctx)


def _na_window_start(rb):
    return jnp.clip(NA_QROWS * rb - NA_WIN_H // 2, 0, GRID_H - NA_KROWS)


def _na_kernel(q_ref, k_ref, v_ref, kc_ref, vc_ref, bias_ref, _, o_ref):
    rb = pl.program_id(2)
    start = pl.multiple_of(_na_window_start(rb) * GRID_W, GRID_W)
    kl = k_ref[pl.ds(start, NA_NLOC), :]
    vl = v_ref[pl.ds(start, NA_NLOC), :]
    kc = kc_ref[0, 0].astype(BF16)
    vc = vc_ref[0, 0].astype(BF16)
    q = q_ref[...] * (NA_DK ** -0.5)
    o = _softmax_pair(q, [(kl, vl), (kc, vc)], [lambda h: bias_ref[0, h], None])
    o_ref[...] = o.astype(o_ref.dtype)


def _na_situation(rb):
    nrb = GRID_H // NA_QROWS
    return jnp.where(rb < 2, rb, jnp.where(rb >= nrb - 2, rb - (nrb - 5), 2))


def neighbourhood_attention_dec(pc_dec, cache_k, cache_v, layer, bias, into):
    hp = NA_HEADS // 2
    nrb = GRID_H // NA_QROWS
    ob0 = M_CTX // NA_QBLK
    return pl.pallas_call(
        _na_kernel,
        grid=(DEC_BATCH, hp, nrb),
        in_specs=[pl.BlockSpec((NA_QBLK, LANES), lambda s, p, r: (s * nrb + r, p)),
                  pl.BlockSpec((DEC_SEQ, LANES), lambda s, p, r: (s, hp + p)),
                  pl.BlockSpec((DEC_SEQ, LANES), lambda s, p, r: (s, 2 * hp + p)),
                  pl.BlockSpec((1, 1, PAST_LEN, LANES), lambda s, p, r: (s, layer, 0, p)),
                  pl.BlockSpec((1, 1, PAST_LEN, LANES), lambda s, p, r: (s, layer, 0, p)),
                  pl.BlockSpec((1, 2, NA_QBLK, NA_NLOC), lambda s, p, r: (_na_situation(r), p, 0, 0)),
                  pl.BlockSpec(memory_space=pl.ANY)],
        out_specs=pl.BlockSpec((NA_QBLK, LANES), lambda s, p, r: (ob0 + s * nrb + r, p)),
        out_shape=jax.ShapeDtypeStruct((M_ALL, NA_W), BF16),
        input_output_aliases={6: 0},
        compiler_params=_params(("arbitrary", "arbitrary", "arbitrary")),
        name="neighbourhood_attention",
    )(pc_dec, pc_dec, pc_dec, cache_k, cache_v, bias, into)


def _na_bias_tables(rpb):
    nrb = GRID_H // NA_QROWS
    rbs = [0, 1, 2, nrb - 2, nrb - 1]
    qc = np.arange(GRID_W)
    kc = np.arange(GRID_W)
    col_start = np.clip(qc - NA_WIN_W // 2, 0, GRID_W - NA_WIN_W)
    col_ok = (kc[None, :] >= col_start[:, None]) & (kc[None, :] < col_start[:, None] + NA_WIN_W)
    dcol = np.clip(kc[None, :] - qc[:, None], 1 - NA_WIN_W, NA_WIN_W - 1) + NA_WIN_W - 1
    drow_all, ok_all = [], []
    for rb in rbs:
        ws = int(np.clip(NA_QROWS * rb - NA_WIN_H // 2, 0, GRID_H - NA_KROWS))
        qr = NA_QROWS * rb + np.arange(NA_QROWS)
        kr = ws + np.arange(NA_KROWS)
        win = np.clip(qr - NA_WIN_H // 2, 0, GRID_H - NA_WIN_H)
        row_ok = (kr[None, :] >= win[:, None]) & (kr[None, :] < win[:, None] + NA_WIN_H)
        drow_all.append(np.clip(kr[None, :] - qr[:, None] + NA_WIN_H - 1, 0, 2 * NA_WIN_H - 2))
        ok_all.append(row_ok[:, None, :, None] & col_ok[None, :, None, :])
    nd, nc = 2 * NA_WIN_H - 1, 2 * NA_WIN_W - 1
    sel_row = (np.stack(drow_all).reshape(-1)[:, None] == np.arange(nd)[None, :]).astype(np.float32)
    sel_col = (np.arange(nc)[:, None] == dcol.reshape(-1)[None, :]).astype(np.float32)
    tab = jnp.einsum("nd,hdc,cx->hnx", sel_row, rpb.astype(F32), sel_col, precision=lax.Precision.HIGHEST)
    tab = tab.reshape(NA_HEADS, 5, NA_QROWS, NA_KROWS, GRID_W, GRID_W)
    tab = jnp.transpose(tab, (1, 0, 2, 4, 3, 5))
    ok = np.stack(ok_all)[:, None]
    return jnp.where(ok, tab, NEG_INF).reshape(5, NA_HEADS, NA_QBLK, NA_NLOC)


def _rope_tables():
    pos = np.arange(DEC_SEQ)
    row = (pos // GRID_W).astype(np.float32)
    col = (pos % GRID_W).astype(np.float32)
    inv = jnp.asarray(ROPE_BASE, F32) ** (-jnp.arange(ROPE_NFREQ, dtype=F32) / ROPE_NFREQ)
    ang_r = jnp.asarray(row)[:, None] * inv
    ang_c = jnp.asarray(col)[:, None] * inv
    cos64 = jnp.concatenate([jnp.cos(ang_r), jnp.cos(ang_r), jnp.cos(ang_c), jnp.cos(ang_c)], axis=1)
    sin64 = jnp.concatenate([-jnp.sin(ang_r), jnp.sin(ang_r), -jnp.sin(ang_c), jnp.sin(ang_c)], axis=1)
    return jnp.tile(cos64, (1, 2)), jnp.tile(sin64, (1, 2))


def kernel(x_prompt, x_sample, c, cache_diff_k, cache_diff_v, cache_na_k, cache_na_v, state_gla_fwd,
           state_gla_bwd, c_ctx, w_ada, b_ada, norm_mix_g, w_in, w_decay, b_decay, gla_norm_g, diff_lambda,
           diff_norm_g, na_rpb, w_br_a, w_br_b, w_br_c, w_out, norm_ffn_g, w_ffn_gate, w_ffn_up, w_ffn_down,
           norm_final_g):
    D = D_MODEL
    xs = (x_prompt.reshape(M_CTX, D), x_sample.reshape(M_DEC, D))
    cond8 = jnp.concatenate([c_ctx[None], c, jnp.zeros((8 - N_COND, D), F32)], axis=0)
    mod = modulation_all(cond8, w_ada, b_ada)
    cos_t, sin_t = _rope_tables()
    g_mix = norm_mix_g.reshape(DEPTH, 1, D)
    g_ffn = norm_ffn_g.reshape(DEPTH, 1, D)
    g_gla = gla_norm_g.reshape(DEPTH, 1, GLA_DV)
    g_diff = diff_norm_g.reshape(DEPTH, 1, DIFF_DV)
    ck_diff = cache_diff_k.reshape(DEC_BATCH, DEPTH, PAST_LEN, DIFF_QK_W)
    cv_diff = cache_diff_v.reshape(DEC_BATCH, DEPTH, PAST_LEN, DIFF_V_W)
    ck_na = cache_na_k.reshape(DEC_BATCH, DEPTH, PAST_LEN, NA_W)
    cv_na = cache_na_v.reshape(DEC_BATCH, DEPTH, PAST_LEN, NA_W)
    gla_state = (state_gla_fwd.reshape(DEC_BATCH, DEPTH, 2, LANES, LANES),
                 state_gla_bwd.reshape(DEC_BATCH, DEPTH, 2, LANES, LANES))

    offs = np.cumsum((0,) + IN_WIDTHS)
    a_end, b_end, c_end = offs[5], offs[8], offs[11]
    new_state = []
    for l in range(DEPTH):
        lam_init = 0.8 - 0.6 * float(np.exp(-0.3 * l))
        m6 = mod[l, :N_COND].reshape(N_COND, 6, 1, D)
        sh_a, sc_a, gt_a, sh_f, sc_f, gt_f = (m6[:, i] for i in range(6))
        wl = w_in[l]
        w_a = jnp.concatenate([wl[:, :a_end], jnp.zeros((D, GLA_A_W - a_end), F32)], axis=1).astype(BF16)
        w_b = wl[:, a_end:b_end].astype(BF16)
        w_c = wl[:, b_end:c_end].astype(BF16)
        w_g = wl[:, c_end:].astype(BF16)

        h = norm_mod(xs, g_mix, l, sc_a, sh_a)
        pa = matmul(h, w_a, F32, tm=512, tn=GLA_A_W, name="in_proj_gla")
        pb_ctx = matmul(h, w_b, F32, rows=M_CTX, name="in_proj_diff_ctx")
        pb_dec = matmul(h, w_b, BF16, row0=M_CTX, rows=M_DEC, name="in_proj_diff_dec")
        pc_ctx = matmul(h, w_c, F32, rows=M_CTX, tn=768, name="in_proj_na_ctx")
        pc_dec = matmul(h, w_c, BF16, row0=M_CTX, rows=M_DEC, tn=768, name="in_proj_na_dec")
        gates = matmul(h, w_g, BF16, act="sigmoid", name="in_proj_gates")

        wd = jnp.zeros((2, LANES, GLA_QK_W), F32)
        wd = wd.at[0, :GLA_LR].set(w_decay[l, 0]).at[1, GLA_LR:2 * GLA_LR].set(w_decay[l, 1])
        bd = b_decay[l][:, None, :]
        oa, s_fin = gla_mixer(pa, wd, bd, g_gla, l, row0=0, nseq=BATCH, seqlen=SEQ)
        oa, _ = gla_mixer(pa, wd, bd, g_gla, l, row0=M_CTX, nseq=DEC_BATCH, seqlen=DEC_SEQ,
                          state=gla_state, into=oa)

        lp = diff_lambda[l].astype(F32)
        lam = (jnp.exp(jnp.sum(lp[0] * lp[1])) - jnp.exp(jnp.sum(lp[2] * lp[3])) + lam_init).reshape(1)
        ob = diff_attention_ctx(pb_ctx, lam, g_diff, l, lam_init)
        qk_rope = rope_qk(pb_dec, cos_t, sin_t)
        ob = diff_attention_dec(qk_rope, pb_dec, ck_diff, cv_diff, l, lam, g_diff, lam_init, ob)

        oc = context_attention_ctx(pc_ctx)
        oc = neighbourhood_attention_dec(pc_dec, ck_na, cv_na, l, _na_bias_tables(na_rpb[l]), oc)

        merged = branch_merge(oa, ob, oc, gates, w_br_a, w_br_b, w_br_c, l)
        x = resid_matmul(merged, w_out, xs, gt_a, l, tm=512, tn=1024, name="out_proj")
        xs = (x,)

        h2 = norm_mod(xs, g_ffn, l, sc_f, sh_f)
        act = ffn_up(h2, w_ffn_gate, w_ffn_up, l)
        x = resid_matmul(act, w_ffn_down, xs, gt_f, l, tm=512, tn=512, name="ffn_down")
        xs = (x,)

        new_state.append((
            pb_ctx[:, DIFF_QK_W:2 * DIFF_QK_W].reshape(BATCH, SEQ, DIFF_HEADS, 2 * DIFF_DK),
            pb_ctx[:, 2 * DIFF_QK_W:].reshape(BATCH, SEQ, DIFF_HEADS, DIFF_DV),
            pc_ctx[:, NA_W:2 * NA_W].reshape(BATCH, SEQ, NA_HEADS, NA_DK),
            pc_ctx[:, 2 * NA_W:].reshape(BATCH, SEQ, NA_HEADS, NA_DK),
            s_fin[0].reshape(BATCH, GLA_HEADS, GLA_DK, GLA_DV),
            s_fin[1].reshape(BATCH, GLA_HEADS, GLA_DK, GLA_DV)))

    gfin = norm_final_g[None]
    y_prompt = final_norm(x, gfin, 0, M_CTX).reshape(BATCH, SEQ, D)
    y_sample = final_norm(x, gfin, M_CTX, M_DEC).reshape(DEC_BATCH, DEC_SEQ, D)
    stacked = tuple(jnp.stack([st[i] for st in new_state], axis=1) for i in range(6))
    return (y_prompt, y_sample) + stacked
```

```python
import functools

import numpy as np
import jax
import jax.numpy as jnp
from jax import lax
from jax.experimental import pallas as pl
from jax.experimental.pallas import tpu as pltpu

F32 = jnp.float32
BF16 = jnp.bfloat16

D_MODEL = 2048
BATCH = 16
SEQ = 256
DEPTH = 2
DEC_BATCH = 2
DEC_SEQ = 4096
PAST_LEN = 512
GRID_W = 64
GRID_H = DEC_SEQ // GRID_W
NORM_EPS = 1e-6
NEG_INF = -1e30
GLA_HEADS = 4
GLA_DK = 64
GLA_DV = 128
GLA_LR = 16
GLA_TAU = 16.0
GLA_CHUNK = 64
GLA_QK_W = GLA_HEADS * GLA_DK
GLA_V_W = GLA_HEADS * GLA_DV
DIFF_HEADS = 8
DIFF_DK = 64
DIFF_DV = 128
DIFF_QK_W = DIFF_HEADS * 2 * DIFF_DK
DIFF_V_W = DIFF_HEADS * DIFF_DV
ROPE_BASE = 10000.0
ROPE_NFREQ = DIFF_DK // 4
NA_HEADS = 8
NA_DK = 64
NA_W = NA_HEADS * NA_DK
NA_WIN_H = 8
NA_WIN_W = 16
D_FF = ((8 * D_MODEL // 3 + 255) // 256) * 256
IN_WIDTHS = (GLA_QK_W, GLA_QK_W, GLA_V_W, GLA_V_W, 2 * GLA_LR,
             DIFF_QK_W, DIFF_QK_W, DIFF_V_W, NA_W, NA_W, NA_W, 3 * D_MODEL)

M_CTX = BATCH * SEQ
M_DEC = DEC_BATCH * DEC_SEQ
M_ALL = M_CTX + M_DEC
N_COND = 1 + DEC_BATCH

LANES = 128
VMEM_LIMIT_BYTES = 56 * 1024 * 1024

GLA_A_W = 13 * LANES
GLA_BLOCK = 256
NA_QROWS = 2
NA_QBLK = NA_QROWS * GRID_W
NA_KROWS = NA_WIN_H + NA_QROWS - 1
NA_NLOC = NA_KROWS * GRID_W
DIFF_KCHUNK = 512
LOG2E = 1.4426950408889634
DIFF_QSCALE = DIFF_DK ** -0.5 * LOG2E


def _params(sem):
    return pltpu.CompilerParams(dimension_semantics=sem, vmem_limit_bytes=VMEM_LIMIT_BYTES)


def _cond_of_row(row):
    return jnp.where(row < M_CTX, 0, 1 + (row - M_CTX) // DEC_SEQ)


def _dot(a, b):
    return jnp.dot(a, b, preferred_element_type=F32)


def _dot_nt(a, b):
    return lax.dot_general(a, b, (((1,), (1,)), ((), ())), preferred_element_type=F32)


def _dot_tn(a, b):
    return lax.dot_general(a, b, (((0,), (0,)), ((), ())), preferred_element_type=F32)


def _split3(x):
    hi = x.astype(BF16)
    r1 = x - hi.astype(F32)
    mid = r1.astype(BF16)
    lo = (r1 - mid.astype(F32)).astype(BF16)
    return hi, mid, lo


def _mod_kernel(c_ref, w_ref, b_ref, o_ref):
    c = c_ref[...]
    a = (c * jax.nn.sigmoid(c)).astype(BF16)
    o_ref[0] = _dot(a, w_ref[0].astype(BF16)) + b_ref[0]


def modulation_all(cond8, w_ada, b_ada):
    n = w_ada.shape[-1]
    tn = 1024
    return pl.pallas_call(
        _mod_kernel,
        grid=(DEPTH, n // tn),
        in_specs=[pl.BlockSpec((8, D_MODEL), lambda l, j: (0, 0)),
                  pl.BlockSpec((1, D_MODEL, tn), lambda l, j: (l, 0, j)),
                  pl.BlockSpec((1, 1, tn), lambda l, j: (l, 0, j))],
        out_specs=pl.BlockSpec((1, 8, tn), lambda l, j: (l, 0, j)),
        out_shape=jax.ShapeDtypeStruct((DEPTH, 8, n), F32),
        compiler_params=_params(("arbitrary", "arbitrary")),
        name="modulation",
    )(cond8, w_ada, b_ada.reshape(DEPTH, 1, n))


def _norm_mod_kernel(xa_ref, xb_ref, g_ref, sc_ref, sh_ref, o_ref, *, n_a):
    def body(x_ref):
        x = x_ref[...]
        y = x * lax.rsqrt(jnp.mean(x * x, axis=-1, keepdims=True) + NORM_EPS) * g_ref[0]
        o_ref[...] = (y * (1.0 + sc_ref[0]) + sh_ref[0]).astype(o_ref.dtype)

    i = pl.program_id(0)

    @pl.when(i < n_a)
    def _():
        body(xa_ref)

    @pl.when(i >= n_a)
    def _():
        body(xb_ref)


def norm_mod(xs, g, layer, sc, sh, tm=512):
    n_a = M_CTX // tm
    amap, bmap = _two_source_maps(n_a, len(xs) == 1)
    xa, xb = (xs[0], xs[0]) if len(xs) == 1 else xs
    cmap = lambda i: (_cond_of_row(i * tm), 0, 0)
    return pl.pallas_call(
        functools.partial(_norm_mod_kernel, n_a=n_a),
        grid=(M_ALL // tm,),
        in_specs=[pl.BlockSpec((tm, D_MODEL), lambda i: (amap(i), 0)),
                  pl.BlockSpec((tm, D_MODEL), lambda i: (bmap(i), 0)),
                  pl.BlockSpec((1, 1, D_MODEL), lambda i: (layer, 0, 0)),
                  pl.BlockSpec((1, 1, D_MODEL), cmap),
                  pl.BlockSpec((1, 1, D_MODEL), cmap)],
        out_specs=pl.BlockSpec((tm, D_MODEL), lambda i: (i, 0)),
        out_shape=jax.ShapeDtypeStruct((M_ALL, D_MODEL), BF16),
        compiler_params=_params(("arbitrary",)),
        name="norm_mod",
    )(xa, xb, g, sc, sh)


def _rmsnorm_kernel(x_ref, g_ref, o_ref):
    x = x_ref[...]
    o_ref[...] = x * lax.rsqrt(jnp.mean(x * x, axis=-1, keepdims=True) + NORM_EPS) * g_ref[...]


def final_norm(x, g, row0, rows, tm=512):
    return pl.pallas_call(
        _rmsnorm_kernel,
        grid=(rows // tm,),
        in_specs=[pl.BlockSpec((tm, D_MODEL), lambda i: (i + row0 // tm, 0)),
                  pl.BlockSpec((1, D_MODEL), lambda i: (0, 0))],
        out_specs=pl.BlockSpec((tm, D_MODEL), lambda i: (i, 0)),
        out_shape=jax.ShapeDtypeStruct((rows, D_MODEL), F32),
        compiler_params=_params(("arbitrary",)),
        name="final_norm",
    )(x, g)


def _mm_kernel(a_ref, w_ref, o_ref, *, act):
    acc = _dot(a_ref[...], w_ref[...])
    if act == "sigmoid":
        acc = jax.nn.sigmoid(acc)
    o_ref[...] = acc.astype(o_ref.dtype)


def matmul(a, w, out_dtype, *, row0=0, rows=None, tm=1024, tn=1024, act=None, name="matmul"):
    k = a.shape[1]
    n = w.shape[1]
    rows = a.shape[0] if rows is None else rows
    tn = min(tn, n)
    assert rows % tm == 0 and row0 % tm == 0 and n % tn == 0
    r0 = row0 // tm
    return pl.pallas_call(
        functools.partial(_mm_kernel, act=act),
        grid=(rows // tm, n // tn),
        in_specs=[pl.BlockSpec((tm, k), lambda i, j: (i + r0, 0)),
                  pl.BlockSpec((k, tn), lambda i, j: (0, j))],
        out_specs=pl.BlockSpec((tm, tn), lambda i, j: (i, j)),
        out_shape=jax.ShapeDtypeStruct((rows, n), out_dtype),
        compiler_params=_params(("arbitrary", "arbitrary")),
        name=name,
    )(a, w)


def _stage_weights(i, pairs):
    @pl.when(i == 0)
    def _():
        for w_ref, w_scr in pairs:
            w_scr[...] = w_ref[0].astype(BF16)


def _merge_kernel(oa_ref, ob_ref, oc_ref, ga_ref, gb_ref, gc_ref, wa_ref, wb_ref, wc_ref, o_ref,
                  wa_scr, wb_scr, wc_scr):
    _stage_weights(pl.program_id(1), ((wa_ref, wa_scr), (wb_ref, wb_scr), (wc_ref, wc_scr)))
    y = ga_ref[...].astype(F32) * _dot(oa_ref[...], wa_scr[...])
    y += gb_ref[...].astype(F32) * _dot(ob_ref[...], wb_scr[...])
    y += gc_ref[...].astype(F32) * _dot(oc_ref[...], wc_scr[...])
    o_ref[...] = y.astype(o_ref.dtype)


def branch_merge(oa, ob, oc, gates, wa, wb, wc, layer, tm=1024, tn=1024):
    m = oa.shape[0]
    nb = D_MODEL // tn
    row = lambda j, i: (i, 0)
    col = lambda j, i: (layer, 0, j)
    return pl.pallas_call(
        _merge_kernel,
        grid=(nb, m // tm),
        in_specs=[pl.BlockSpec((tm, GLA_V_W), row),
                  pl.BlockSpec((tm, DIFF_V_W), row),
                  pl.BlockSpec((tm, NA_W), row),
                  pl.BlockSpec((tm, tn), lambda j, i: (i, j)),
                  pl.BlockSpec((tm, tn), lambda j, i: (i, j + nb)),
                  pl.BlockSpec((tm, tn), lambda j, i: (i, j + 2 * nb)),
                  pl.BlockSpec((1, GLA_V_W, tn), col),
                  pl.BlockSpec((1, DIFF_V_W, tn), col),
                  pl.BlockSpec((1, NA_W, tn), col)],
        out_specs=pl.BlockSpec((tm, tn), lambda j, i: (i, j)),
        out_shape=jax.ShapeDtypeStruct((m, D_MODEL), BF16),
        scratch_shapes=[pltpu.VMEM((GLA_V_W, tn), BF16), pltpu.VMEM((DIFF_V_W, tn), BF16),
                        pltpu.VMEM((NA_W, tn), BF16)],
        compiler_params=_params(("arbitrary", "arbitrary")),
        name="branch_merge",
    )(oa, ob, oc, gates, gates, gates, wa, wb, wc)


def _two_source_maps(n_a, stacked):
    b_off = n_a if stacked else 0
    return (lambda i: jnp.minimum(i, n_a - 1)), (lambda i: jnp.maximum(i - n_a, 0) + b_off)


def _resid_mm_kernel(a_ref, w_ref, xa_ref, xb_ref, gt_ref, o_ref, w_scr, *, n_a):
    i = pl.program_id(1)
    _stage_weights(i, ((w_ref, w_scr),))
    y = gt_ref[0] * _dot(a_ref[...], w_scr[...])

    @pl.when(i < n_a)
    def _():
        o_ref[...] = xa_ref[...] + y

    @pl.when(i >= n_a)
    def _():
        o_ref[...] = xb_ref[...] + y


def resid_matmul(a, w, xs, gt, layer, tm, tn, name):
    m, k = a.shape
    n = w.shape[2]
    n_a = M_CTX // tm
    amap, bmap = _two_source_maps(n_a, len(xs) == 1)
    xa, xb = (xs[0], xs[0]) if len(xs) == 1 else xs
    return pl.pallas_call(
        functools.partial(_resid_mm_kernel, n_a=n_a),
        grid=(n // tn, m // tm),
        in_specs=[pl.BlockSpec((tm, k), lambda j, i: (i, 0)),
                  pl.BlockSpec((1, k, tn), lambda j, i: (layer, 0, j)),
                  pl.BlockSpec((tm, tn), lambda j, i: (amap(i), j)),
                  pl.BlockSpec((tm, tn), lambda j, i: (bmap(i), j)),
                  pl.BlockSpec((1, 1, tn), lambda j, i: (_cond_of_row(i * tm), 0, j))],
        out_specs=pl.BlockSpec((tm, tn), lambda j, i: (i, j)),
        out_shape=jax.ShapeDtypeStruct((m, n), F32),
        scratch_shapes=[pltpu.VMEM((k, tn), BF16)],
        compiler_params=_params(("arbitrary", "arbitrary")),
        name=name,
    )(a, w, xa, xb, gt)


def _ffn_up_kernel(h_ref, wg_ref, wu_ref, o_ref, wg_scr, wu_scr):
    _stage_weights(pl.program_id(1), ((wg_ref, wg_scr), (wu_ref, wu_scr)))
    h = h_ref[...]
    g = _dot(h, wg_scr[...])
    u = _dot(h, wu_scr[...])
    o_ref[...] = (g * jax.nn.sigmoid(g) * u).astype(o_ref.dtype)


def ffn_up(h, wg, wu, layer, tm=1024, tn=512):
    m = h.shape[0]
    wspec = pl.BlockSpec((1, D_MODEL, tn), lambda j, i: (layer, 0, j))
    return pl.pallas_call(
        _ffn_up_kernel,
        grid=(D_FF // tn, m // tm),
        in_specs=[pl.BlockSpec((tm, D_MODEL), lambda j, i: (i, 0)), wspec, wspec],
        out_specs=pl.BlockSpec((tm, tn), lambda j, i: (i, j)),
        out_shape=jax.ShapeDtypeStruct((m, D_FF), BF16),
        scratch_shapes=[pltpu.VMEM((D_MODEL, tn), BF16), pltpu.VMEM((D_MODEL, tn), BF16)],
        compiler_params=_params(("arbitrary", "arbitrary")),
        name="ffn_up",
    )(h, wg, wu)


def _gla_kernel(q_ref, k_ref, v_ref, gr_ref, lr_ref, wd_ref, bd_ref, g_ref, *rest, nblk, has_state):
    if has_state:
        s0f_ref, s0b_ref, _, o_ref, sfin_ref, s_scr, of_scr = rest
    else:
        o_ref, sfin_ref, s_scr, of_scr = rest
    pss = pl.program_id(1)
    blk = pl.program_id(2)
    C = GLA_CHUNK
    nch = GLA_BLOCK // C

    if has_state:
        @pl.when(jnp.logical_and(blk == 0, pss == 0))
        def _init_f():
            s_scr[...] = s0f_ref[0, 0]

        @pl.when(jnp.logical_and(blk == 0, pss == 1))
        def _init_b():
            s_scr[...] = s0b_ref[0, 0]
    else:
        @pl.when(blk == 0)
        def _init():
            s_scr[...] = jnp.zeros_like(s_scr)

    lr = lr_ref[...]
    lr_hi = lr.astype(BF16)
    lr_lo = (lr - lr_hi.astype(F32)).astype(BF16)
    wd = wd_ref[0]
    wd_hi = wd.astype(BF16)
    wd_lo = (wd - wd_hi.astype(F32)).astype(BF16)
    pre = _dot(lr_hi, wd_hi) + _dot(lr_hi, wd_lo) + _dot(lr_lo, wd_hi) + bd_ref[0]
    la_all = jax.nn.log_sigmoid(pre) * (1.0 / GLA_TAU)

    row = lax.broadcasted_iota(jnp.int32, (C, C), 0)
    colm = lax.broadcasted_iota(jnp.int32, (C, C), 1)
    lane = lax.broadcasted_iota(jnp.int32, (1, LANES), 1)
    head_mask = (lane < GLA_DK, lane >= GLA_DK)
    eye = (lax.broadcasted_iota(jnp.int32, (LANES, LANES), 0)
           == lax.broadcasted_iota(jnp.int32, (LANES, LANES), 1))

    def run_direction(backward):
        keep = (colm >= row) if backward else (colm <= row)
        tri = jnp.where(keep, 1.0, 0.0).astype(BF16)
        last, mid = (0, C // 2) if backward else (C - 1, C // 2 - 1)
        order = range(nch - 1, -1, -1) if backward else range(nch)
        for c in order:
            rs = slice(c * C, (c + 1) * C)
            for p in range(GLA_HEADS // 2):
                ls = slice(p * LANES, (p + 1) * LANES)
                la = la_all[rs, ls]
                hi, md, lo = _split3(la)
                b = _dot(tri, hi) + _dot(tri, md) + _dot(tri, lo)
                b_last = b[last:last + 1, :]
                b_mid = b[mid:mid + 1, :]
                q = q_ref[rs, ls] * (GLA_DK ** -0.5)
                k = k_ref[rs, ls]
                q_in = q * jnp.exp(b - b_mid)
                k_in = (k * jnp.exp(b_mid - b)).astype(BF16)
                q_ex = q * jnp.exp(b)
                k_dc = k * jnp.exp(b_last - b)
                dec_col = jnp.exp(jnp.sum(jnp.where(eye, b_last, 0.0), axis=1, keepdims=True))
                s_old = s_scr[p]
                s_old_bf = s_old.astype(BF16)
                s_new = dec_col * s_old
                for h in range(2):
                    hd = 2 * p + h
                    vs = slice(hd * GLA_DV, (hd + 1) * GLA_DV)
                    v = v_ref[rs, vs].astype(BF16)
                    att = _dot_nt(jnp.where(head_mask[h], q_in, 0.0).astype(BF16), k_in)
                    att = jnp.where(keep, att, 0.0).astype(BF16)
                    o = _dot(att, v) + _dot(jnp.where(head_mask[h], q_ex, 0.0).astype(BF16), s_old_bf)
                    s_new = s_new + _dot_tn(jnp.where(head_mask[h], k_dc, 0.0).astype(BF16), v)
                    if backward:
                        tot = of_scr[blk_rows(c), vs] + o
                        y = tot * lax.rsqrt(jnp.mean(tot * tot, axis=-1, keepdims=True) + NORM_EPS) * g_ref[0]
                        gr = gr_ref[rs, vs]
                        o_ref[rs, vs] = (y * (gr * jax.nn.sigmoid(gr))).astype(o_ref.dtype)
                    else:
                        of_scr[blk_rows(c), vs] = o
                s_scr[p] = s_new

    def blk_rows(c):
        pos = jnp.where(pss == 0, blk, nblk - 1 - blk)
        return pl.ds(pl.multiple_of(pos * GLA_BLOCK + c * C, C), C)

    @pl.when(pss == 0)
    def _fwd():
        run_direction(False)

    @pl.when(pss == 1)
    def _bwd():
        run_direction(True)

    @pl.when(blk == nblk - 1)
    def _fin():
        sfin_ref[0, 0] = s_scr[...]


def gla_mixer(pa, wd, bd, g, layer, *, row0, nseq, seqlen, state=None, into=None):
    nblk = seqlen // GLA_BLOCK
    b0 = row0 // GLA_BLOCK
    has_state = state is not None
    assert has_state == (into is not None)

    def rb(s, p, b):
        return b0 + s * nblk + b + p * (nblk - 1 - 2 * b)

    def orb(s, p, b):
        return b0 + s * nblk + nblk - 1 - p * b

    in_specs = [pl.BlockSpec((GLA_BLOCK, GLA_QK_W), lambda s, p, b: (rb(s, p, b), 0)),
                pl.BlockSpec((GLA_BLOCK, GLA_QK_W), lambda s, p, b: (rb(s, p, b), 1)),
                pl.BlockSpec((GLA_BLOCK, GLA_V_W), lambda s, p, b: (rb(s, p, b), 1)),
                pl.BlockSpec((GLA_BLOCK, GLA_V_W), lambda s, p, b: (rb(s, p, b), 2)),
                pl.BlockSpec((GLA_BLOCK, LANES), lambda s, p, b: (rb(s, p, b), GLA_A_W // LANES - 1)),
                pl.BlockSpec((1, LANES, GLA_QK_W), lambda s, p, b: (p, 0, 0)),
                pl.BlockSpec((1, 1, GLA_QK_W), lambda s, p, b: (p, 0, 0)),
                pl.BlockSpec((1, 1, GLA_DV), lambda s, p, b: (layer, 0, 0))]
    args = [pa, pa, pa, pa, pa, wd, bd, g]
    aliases = {}
    if has_state:
        st_spec = pl.BlockSpec((1, 1, 2, LANES, LANES), lambda s, p, b: (s, layer, 0, 0, 0))
        in_specs += [st_spec, st_spec, pl.BlockSpec(memory_space=pl.ANY)]
        args += [state[0], state[1], into]
        aliases = {len(args) - 1: 0}
    kern = functools.partial(_gla_kernel, nblk=nblk, has_state=has_state)
    return pl.pallas_call(
        kern,
        grid=(nseq, 2, nblk),
        in_specs=in_specs,
        out_specs=[pl.BlockSpec((GLA_BLOCK, GLA_V_W), lambda s, p, b: (orb(s, p, b), 0)),
                   pl.BlockSpec((1, 1, 2, LANES, LANES), lambda s, p, b: (p, s, 0, 0, 0))],
        out_shape=[jax.ShapeDtypeStruct((M_ALL, GLA_V_W), BF16),
                   jax.ShapeDtypeStruct((2, nseq, 2, LANES, LANES), F32)],
        scratch_shapes=[pltpu.VMEM((2, LANES, LANES), F32),
                        pltpu.VMEM((seqlen, GLA_V_W), F32)],
        input_output_aliases=aliases,
        compiler_params=_params(("arbitrary", "arbitrary", "arbitrary")),
        name="gla_mixer",
    )(*args)


def _rope_kernel(x_ref, cos_ref, sin_ref, o_ref):
    cos = cos_ref[...]
    sin = sin_ref[...]
    lane = lax.broadcasted_iota(jnp.int32, (1, LANES), 1)
    first = (lane % (2 * ROPE_NFREQ)) < ROPE_NFREQ
    nq = DIFF_QK_W // LANES
    for j in range(2 * nq):
        x = x_ref[:, j * LANES:(j + 1) * LANES].astype(F32)
        partner = jnp.where(first, pltpu.roll(x, LANES - ROPE_NFREQ, 1), pltpu.roll(x, ROPE_NFREQ, 1))
        y = x * cos + partner * sin
        if j < nq:
            y = y * DIFF_QSCALE
        o_ref[:, j * LANES:(j + 1) * LANES] = y.astype(o_ref.dtype)


def rope_qk(pb_dec, cos_t, sin_t, tm=512):
    nt = DEC_SEQ // tm
    return pl.pallas_call(
        _rope_kernel,
        grid=(M_DEC // tm,),
        in_specs=[pl.BlockSpec((tm, 2 * DIFF_QK_W), lambda i: (i, 0)),
                  pl.BlockSpec((tm, LANES), lambda i: (i % nt, 0)),
                  pl.BlockSpec((tm, LANES), lambda i: (i % nt, 0))],
        out_specs=pl.BlockSpec((tm, 2 * DIFF_QK_W), lambda i: (i, 0)),
        out_shape=jax.ShapeDtypeStruct((M_DEC, 2 * DIFF_QK_W), BF16),
        compiler_params=_params(("arbitrary",)),
        name="rope_qk",
    )(pb_dec, cos_t, sin_t)


def _diff_head(q, chunks, lam, g, lam_init):
    lane = lax.broadcasted_iota(jnp.int32, (1, LANES), 1)
    qm = [jnp.where(lane < DIFF_DK, q, jnp.zeros_like(q)), jnp.where(lane >= DIFF_DK, q, jnp.zeros_like(q))]
    m = [None, None]
    acc = [None, None]
    for k, v in chunks:
        for mp in range(2):
            s = _dot_nt(qm[mp], k)
            cm = jnp.max(s, axis=-1, keepdims=True)
            if m[mp] is None:
                m[mp] = cm
                acc[mp] = _dot(jnp.exp2(s - cm).astype(BF16), v)
            else:
                m_new = jnp.maximum(m[mp], cm)
                acc[mp] = jnp.exp2(m[mp] - m_new) * acc[mp] + _dot(jnp.exp2(s - m_new).astype(BF16), v)
                m[mp] = m_new
    outs = [a[:, :LANES] / a[:, LANES:] for a in acc]
    o = outs[0] - lam * outs[1]
    y = o * lax.rsqrt(jnp.mean(o * o, axis=-1, keepdims=True) + NORM_EPS) * g
    return y * (1.0 - lam_init)


def _diff_kernel(lam_ref, q_ref, kn_ref, vn_ref, kc_ref, vc_ref, g_ref, _, o_ref, vaug, kc_scr, *, lam_init):
    n_new = kn_ref.shape[0]
    n_c = PAST_LEN

    @pl.when(pl.program_id(2) == 0)
    def _stage():
        vaug[:, LANES:] = jnp.ones((n_c + n_new, LANES), BF16)
        vaug[n_c:, :LANES] = vn_ref[...]
        vaug[:n_c, :LANES] = vc_ref[0, 0].astype(BF16)
        kc_scr[...] = kc_ref[0, 0].astype(BF16)

    chunks = [(kc_scr[...], vaug[:n_c, :])]
    for c in range(n_new // DIFF_KCHUNK):
        rows = slice(c * DIFF_KCHUNK, (c + 1) * DIFF_KCHUNK)
        chunks.append((kn_ref[rows, :], vaug[n_c + c * DIFF_KCHUNK:n_c + (c + 1) * DIFF_KCHUNK, :]))
    y = _diff_head(q_ref[...], chunks, lam_ref[0], g_ref[0], lam_init)
    o_ref[...] = y.astype(o_ref.dtype)


def _diff_ctx_kernel(lam_ref, q_ref, k_ref, v_ref, g_ref, o_ref, *, lam_init):
    ones = jnp.ones((SEQ, LANES), BF16)
    for h in range(DIFF_HEADS):
        ls = slice(h * LANES, (h + 1) * LANES)
        q = (q_ref[:, ls] * DIFF_QSCALE).astype(BF16)
        vaug = jnp.concatenate([v_ref[:, ls].astype(BF16), ones], axis=1)
        y = _diff_head(q, [(k_ref[:, ls].astype(BF16), vaug)], lam_ref[0], g_ref[0], lam_init)
        o_ref[:, ls] = y.astype(o_ref.dtype)


def diff_attention_ctx(pb_ctx, lam, g, layer, lam_init):
    return pl.pallas_call(
        functools.partial(_diff_ctx_kernel, lam_init=lam_init),
        grid=(BATCH,),
        in_specs=[pl.BlockSpec(memory_space=pltpu.SMEM),
                  pl.BlockSpec((SEQ, DIFF_QK_W), lambda s: (s, 0)),
                  pl.BlockSpec((SEQ, DIFF_QK_W), lambda s: (s, 1)),
                  pl.BlockSpec((SEQ, DIFF_V_W), lambda s: (s, 2)),
                  pl.BlockSpec((1, 1, DIFF_DV), lambda s: (layer, 0, 0))],
        out_specs=pl.BlockSpec((SEQ, DIFF_V_W), lambda s: (s, 0)),
        out_shape=jax.ShapeDtypeStruct((M_ALL, DIFF_V_W), BF16),
        compiler_params=_params(("arbitrary",)),
        name="diff_attention_ctx",
    )(lam, pb_ctx, pb_ctx, pb_ctx, g)


def diff_attention_dec(qk_rope, pb_dec, cache_k, cache_v, layer, lam, g, lam_init, into, tq=512):
    kern = functools.partial(_diff_kernel, lam_init=lam_init)
    H = DIFF_HEADS
    nqb = DEC_SEQ // tq
    ntot = PAST_LEN + DEC_SEQ
    ob0 = M_CTX // tq
    return pl.pallas_call(
        kern,
        grid=(DEC_BATCH, H, nqb),
        in_specs=[pl.BlockSpec(memory_space=pltpu.SMEM),
                  pl.BlockSpec((tq, LANES), lambda s, h, b: (s * nqb + b, h)),
                  pl.BlockSpec((DEC_SEQ, LANES), lambda s, h, b: (s, H + h)),
                  pl.BlockSpec((DEC_SEQ, LANES), lambda s, h, b: (s, 2 * H + h)),
                  pl.BlockSpec((1, 1, PAST_LEN, LANES), lambda s, h, b: (s, layer, 0, h)),
                  pl.BlockSpec((1, 1, PAST_LEN, LANES), lambda s, h, b: (s, layer, 0, h)),
                  pl.BlockSpec((1, 1, DIFF_DV), lambda s, h, b: (layer, 0, 0)),
                  pl.BlockSpec(memory_space=pl.ANY)],
        out_specs=pl.BlockSpec((tq, LANES), lambda s, h, b: (ob0 + s * nqb + b, h)),
        out_shape=jax.ShapeDtypeStruct((M_ALL, DIFF_V_W), BF16),
        scratch_shapes=[pltpu.VMEM((ntot, 2 * LANES), BF16),
                        pltpu.VMEM((PAST_LEN, LANES), BF16)],
        input_output_aliases={7: 0},
        compiler_params=_params(("arbitrary", "arbitrary", "arbitrary")),
        name="diff_attention_dec",
    )(lam, qk_rope, qk_rope, pb_dec, cache_k, cache_v, g, into)


def _softmax_pair(q, keys_vals, biases):
    lane = lax.broadcasted_iota(jnp.int32, (1, LANES), 1)
    outs = []
    for h in range(2):
        msk = (lane < NA_DK) if h == 0 else (lane >= NA_DK)
        qm = jnp.where(msk, q, jnp.zeros_like(q))
        ss = []
        for (k, _), bias in zip(keys_vals, biases):
            s = _dot_nt(qm, k)
            if bias is not None:
                s = s + bias(h)
            ss.append(s)
        m = functools.reduce(jnp.maximum, [jnp.max(s, axis=-1, keepdims=True) for s in ss])
        num = 0.0
        den = 0.0
        for s, (_, v) in zip(ss, keys_vals):
            e = jnp.exp(s - m)
            den = den + jnp.sum(e, axis=-1, keepdims=True)
            num = num + _dot(e.astype(BF16), v)
        outs.append(num / den)
    return jnp.where(lane < NA_DK, outs[0], outs[1])


def _ctx_attn_kernel(q_ref, k_ref, v_ref, o_ref):
    for p in range(NA_HEADS // 2):
        ls = slice(p * LANES, (p + 1) * LANES)
        q = (q_ref[:, ls] * (NA_DK ** -0.5)).astype(BF16)
        k = k_ref[:, ls].astype(BF16)
        v = v_ref[:, ls].astype(BF16)
        o_ref[:, ls] = _softmax_pair(q, [(k, v)], [None]).astype(o_ref.dtype)


def context_attention_ctx(pc_ctx):
    return pl.pallas_call(
        _ctx_attn_kernel,
        grid=(BATCH,),
        in_specs=[pl.BlockSpec((SEQ, NA_W), lambda s: (s, 0)),
                  pl.BlockSpec((SEQ, NA_W), lambda s: (s, 1)),
                  pl.BlockSpec((SEQ, NA_W), lambda s: (s, 2))],
        out_specs=pl.BlockSpec((SEQ, NA_W), lambda s: (s, 0)),
        out_shape=jax.ShapeDtypeStruct((M_ALL, NA_W), BF16),
        compiler_params=_params(("arbitrary",)),
        name="context_attention",
    )(pc_ctx, pc_ctx, pc_ctx)


def _na_window_start(rb):
    return jnp.clip(NA_QROWS * rb - NA_WIN_H // 2, 0, GRID_H - NA_KROWS)


def _na_kernel(q_ref, k_ref, v_ref, kc_ref, vc_ref, bias_ref, _, o_ref, kc_scr, vc_scr):
    rb = pl.program_id(1)

    @pl.when(rb == 0)
    def _stage():
        kc_scr[...] = kc_ref[0, 0].astype(BF16)
        vc_scr[...] = vc_ref[0, 0].astype(BF16)

    start = pl.multiple_of(_na_window_start(rb) * GRID_W, GRID_W)
    for p in range(NA_HEADS // 2):
        ls = slice(p * LANES, (p + 1) * LANES)
        kl = k_ref[pl.ds(start, NA_NLOC), ls]
        vl = v_ref[pl.ds(start, NA_NLOC), ls]
        q = q_ref[:, ls] * (NA_DK ** -0.5)
        o = _softmax_pair(q, [(kl, vl), (kc_scr[:, ls], vc_scr[:, ls])],
                          [lambda h, p=p: bias_ref[0, 2 * p + h], None])
        o_ref[:, ls] = o.astype(o_ref.dtype)


def _na_situation(rb):
    nrb = GRID_H // NA_QROWS
    return jnp.where(rb < 2, rb, jnp.where(rb >= nrb - 2, rb - (nrb - 5), 2))


def neighbourhood_attention_dec(pc_dec, cache_k, cache_v, layer, bias, into):
    nrb = GRID_H // NA_QROWS
    ob0 = M_CTX // NA_QBLK
    cspec = pl.BlockSpec((1, 1, PAST_LEN, NA_W), lambda s, r: (s, layer, 0, 0))
    return pl.pallas_call(
        _na_kernel,
        grid=(DEC_BATCH, nrb),
        in_specs=[pl.BlockSpec((NA_QBLK, NA_W), lambda s, r: (s * nrb + r, 0)),
                  pl.BlockSpec((DEC_SEQ, NA_W), lambda s, r: (s, 1)),
                  pl.BlockSpec((DEC_SEQ, NA_W), lambda s, r: (s, 2)),
                  cspec, cspec,
                  pl.BlockSpec((1, NA_HEADS, NA_QBLK, NA_NLOC), lambda s, r: (_na_situation(r), 0, 0, 0)),
                  pl.BlockSpec(memory_space=pl.ANY)],
        out_specs=pl.BlockSpec((NA_QBLK, NA_W), lambda s, r: (ob0 + s * nrb + r, 0)),
        out_shape=jax.ShapeDtypeStruct((M_ALL, NA_W), BF16),
        scratch_shapes=[pltpu.VMEM((PAST_LEN, NA_W), BF16), pltpu.VMEM((PAST_LEN, NA_W), BF16)],
        input_output_aliases={6: 0},
        compiler_params=_params(("arbitrary", "arbitrary")),
        name="neighbourhood_attention",
    )(pc_dec, pc_dec, pc_dec, cache_k, cache_v, bias, into)


def _na_bias_tables(rpb):
    nrb = GRID_H // NA_QROWS
    rbs = [0, 1, 2, nrb - 2, nrb - 1]
    qc = np.arange(GRID_W)
    kc = np.arange(GRID_W)
    col_start = np.clip(qc - NA_WIN_W // 2, 0, GRID_W - NA_WIN_W)
    col_ok = (kc[None, :] >= col_start[:, None]) & (kc[None, :] < col_start[:, None] + NA_WIN_W)
    dcol = np.clip(kc[None, :] - qc[:, None], 1 - NA_WIN_W, NA_WIN_W - 1) + NA_WIN_W - 1
    drow_all, ok_all = [], []
    for rb in rbs:
        ws = int(np.clip(NA_QROWS * rb - NA_WIN_H // 2, 0, GRID_H - NA_KROWS))
        qr = NA_QROWS * rb + np.arange(NA_QROWS)
        kr = ws + np.arange(NA_KROWS)
        win = np.clip(qr - NA_WIN_H // 2, 0, GRID_H - NA_WIN_H)
        row_ok = (kr[None, :] >= win[:, None]) & (kr[None, :] < win[:, None] + NA_WIN_H)
        drow_all.append(np.clip(kr[None, :] - qr[:, None] + NA_WIN_H - 1, 0, 2 * NA_WIN_H - 2))
        ok_all.append(row_ok[:, None, :, None] & col_ok[None, :, None, :])
    nd, nc = 2 * NA_WIN_H - 1, 2 * NA_WIN_W - 1
    sel_row = (np.stack(drow_all).reshape(-1)[:, None] == np.arange(nd)[None, :]).astype(np.float32)
    sel_col = (np.arange(nc)[:, None] == dcol.reshape(-1)[None, :]).astype(np.float32)
    tab = jnp.einsum("nd,hdc,cx->hnx", sel_row, rpb.astype(F32), sel_col, precision=lax.Precision.HIGHEST)
    tab = tab.reshape(NA_HEADS, 5, NA_QROWS, NA_KROWS, GRID_W, GRID_W)
    tab = jnp.transpose(tab, (1, 0, 2, 4, 3, 5))
    ok = np.stack(ok_all)[:, None]
    return jnp.where(ok, tab, NEG_INF).reshape(5, NA_HEADS, NA_QBLK, NA_NLOC)


def _rope_tables():
    pos = np.arange(DEC_SEQ)
    row = (pos // GRID_W).astype(np.float32)
    col = (pos % GRID_W).astype(np.float32)
    inv = jnp.asarray(ROPE_BASE, F32) ** (-jnp.arange(ROPE_NFREQ, dtype=F32) / ROPE_NFREQ)
    ang_r = jnp.asarray(row)[:, None] * inv
    ang_c = jnp.asarray(col)[:, None] * inv
    cos64 = jnp.concatenate([jnp.cos(ang_r), jnp.cos(ang_r), jnp.cos(ang_c), jnp.cos(ang_c)], axis=1)
    sin64 = jnp.concatenate([-jnp.sin(ang_r), jnp.sin(ang_r), -jnp.sin(ang_c), jnp.sin(ang_c)], axis=1)
    return jnp.tile(cos64, (1, 2)), jnp.tile(sin64, (1, 2))


def kernel(x_prompt, x_sample, c, cache_diff_k, cache_diff_v, cache_na_k, cache_na_v, state_gla_fwd,
           state_gla_bwd, c_ctx, w_ada, b_ada, norm_mix_g, w_in, w_decay, b_decay, gla_norm_g, diff_lambda,
           diff_norm_g, na_rpb, w_br_a, w_br_b, w_br_c, w_out, norm_ffn_g, w_ffn_gate, w_ffn_up, w_ffn_down,
           norm_final_g):
    D = D_MODEL
    xs = (x_prompt.reshape(M_CTX, D), x_sample.reshape(M_DEC, D))
    cond8 = jnp.concatenate([c_ctx[None], c, jnp.zeros((8 - N_COND, D), F32)], axis=0)
    mod = modulation_all(cond8, w_ada, b_ada)
    cos_t, sin_t = _rope_tables()
    g_mix = norm_mix_g.reshape(DEPTH, 1, D)
    g_ffn = norm_ffn_g.reshape(DEPTH, 1, D)
    g_gla = gla_norm_g.reshape(DEPTH, 1, GLA_DV)
    g_diff = diff_norm_g.reshape(DEPTH, 1, DIFF_DV)
    ck_diff = cache_diff_k.reshape(DEC_BATCH, DEPTH, PAST_LEN, DIFF_QK_W)
    cv_diff = cache_diff_v.reshape(DEC_BATCH, DEPTH, PAST_LEN, DIFF_V_W)
    ck_na = cache_na_k.reshape(DEC_BATCH, DEPTH, PAST_LEN, NA_W)
    cv_na = cache_na_v.reshape(DEC_BATCH, DEPTH, PAST_LEN, NA_W)
    gla_state = (state_gla_fwd.reshape(DEC_BATCH, DEPTH, 2, LANES, LANES),
                 state_gla_bwd.reshape(DEC_BATCH, DEPTH, 2, LANES, LANES))

    offs = np.cumsum((0,) + IN_WIDTHS)
    a_end, b_end, c_end = offs[5], offs[8], offs[11]
    new_state = []
    for l in range(DEPTH):
        lam_init = 0.8 - 0.6 * float(np.exp(-0.3 * l))
        m6 = mod[l, :N_COND].reshape(N_COND, 6, 1, D)
        sh_a, sc_a, gt_a, sh_f, sc_f, gt_f = (m6[:, i] for i in range(6))
        wl = w_in[l]
        w_a = jnp.concatenate([wl[:, :a_end], jnp.zeros((D, GLA_A_W - a_end), F32)], axis=1).astype(BF16)
        w_b = wl[:, a_end:b_end].astype(BF16)
        w_c = wl[:, b_end:c_end].astype(BF16)
        w_g = wl[:, c_end:].astype(BF16)

        h = norm_mod(xs, g_mix, l, sc_a, sh_a)
        pa = matmul(h, w_a, F32, tm=512, tn=GLA_A_W, name="in_proj_gla")
        pb_ctx = matmul(h, w_b, F32, rows=M_CTX, name="in_proj_diff_ctx")
        pb_dec = matmul(h, w_b, BF16, row0=M_CTX, rows=M_DEC, name="in_proj_diff_dec")
        pc_ctx = matmul(h, w_c, F32, rows=M_CTX, tn=768, name="in_proj_na_ctx")
        pc_dec = matmul(h, w_c, BF16, row0=M_CTX, rows=M_DEC, tn=768, name="in_proj_na_dec")
        gates = matmul(h, w_g, BF16, act="sigmoid", name="in_proj_gates")

        wd = jnp.zeros((2, LANES, GLA_QK_W), F32)
        wd = wd.at[0, :GLA_LR].set(w_decay[l, 0]).at[1, GLA_LR:2 * GLA_LR].set(w_decay[l, 1])
        bd = b_decay[l][:, None, :]
        oa, s_fin = gla_mixer(pa, wd, bd, g_gla, l, row0=0, nseq=BATCH, seqlen=SEQ)
        oa, _ = gla_mixer(pa, wd, bd, g_gla, l, row0=M_CTX, nseq=DEC_BATCH, seqlen=DEC_SEQ,
                          state=gla_state, into=oa)

        lp = diff_lambda[l].astype(F32)
        lam = (jnp.exp(jnp.sum(lp[0] * lp[1])) - jnp.exp(jnp.sum(lp[2] * lp[3])) + lam_init).reshape(1)
        ob = diff_attention_ctx(pb_ctx, lam, g_diff, l, lam_init)
        qk_rope = rope_qk(pb_dec, cos_t, sin_t)
        ob = diff_attention_dec(qk_rope, pb_dec, ck_diff, cv_diff, l, lam, g_diff, lam_init, ob)

        oc = context_attention_ctx(pc_ctx)
        oc = neighbourhood_attention_dec(pc_dec, ck_na, cv_na, l, _na_bias_tables(na_rpb[l]), oc)

        merged = branch_merge(oa, ob, oc, gates, w_br_a, w_br_b, w_br_c, l)
        x = resid_matmul(merged, w_out, xs, gt_a, l, tm=512, tn=1024, name="out_proj")
        xs = (x,)

        h2 = norm_mod(xs, g_ffn, l, sc_f, sh_f)
        act = ffn_up(h2, w_ffn_gate, w_ffn_up, l)
        x = resid_matmul(act, w_ffn_down, xs, gt_f, l, tm=512, tn=512, name="ffn_down")
        xs = (x,)

        new_state.append((
            pb_ctx[:, DIFF_QK_W:2 * DIFF_QK_W].reshape(BATCH, SEQ, DIFF_HEADS, 2 * DIFF_DK),
            pb_ctx[:, 2 * DIFF_QK_W:].reshape(BATCH, SEQ, DIFF_HEADS, DIFF_DV),
            pc_ctx[:, NA_W:2 * NA_W].reshape(BATCH, SEQ, NA_HEADS, NA_DK),
            pc_ctx[:, 2 * NA_W:].reshape(BATCH, SEQ, NA_HEADS, NA_DK),
            s_fin[0].reshape(BATCH, GLA_HEADS, GLA_DK, GLA_DV),
            s_fin[1].reshape(BATCH, GLA_HEADS, GLA_DK, GLA_DV)))

    gfin = norm_final_g[None]
    y_prompt = final_norm(x, gfin, 0, M_CTX).reshape(BATCH, SEQ, D)
    y_sample = final_norm(x, gfin, M_CTX, M_DEC).reshape(DEC_BATCH, DEC_SEQ, D)
    stacked = tuple(jnp.stack([st[i] for st in new_state], axis=1) for i in range(6))
    return (y_prompt, y_sample) + stacked
```

```python
import functools

import numpy as np
import jax
import jax.numpy as jnp
from jax import lax
from jax.experimental import pallas as pl
from jax.experimental.pallas import tpu as pltpu

F32 = jnp.float32
BF16 = jnp.bfloat16

D_MODEL = 2048
BATCH = 16
SEQ = 256
DEPTH = 2
DEC_BATCH = 2
DEC_SEQ = 4096
PAST_LEN = 512
GRID_W = 64
GRID_H = DEC_SEQ // GRID_W
NORM_EPS = 1e-6
NEG_INF = -1e30
GLA_HEADS = 4
GLA_DK = 64
GLA_DV = 128
GLA_LR = 16
GLA_TAU = 16.0
GLA_CHUNK = 64
GLA_QK_W = GLA_HEADS * GLA_DK
GLA_V_W = GLA_HEADS * GLA_DV
DIFF_HEADS = 8
DIFF_DK = 64
DIFF_DV = 128
DIFF_QK_W = DIFF_HEADS * 2 * DIFF_DK
DIFF_V_W = DIFF_HEADS * DIFF_DV
ROPE_BASE = 10000.0
ROPE_NFREQ = DIFF_DK // 4
NA_HEADS = 8
NA_DK = 64
NA_W = NA_HEADS * NA_DK
NA_WIN_H = 8
NA_WIN_W = 16
D_FF = ((8 * D_MODEL // 3 + 255) // 256) * 256
IN_WIDTHS = (GLA_QK_W, GLA_QK_W, GLA_V_W, GLA_V_W, 2 * GLA_LR,
             DIFF_QK_W, DIFF_QK_W, DIFF_V_W, NA_W, NA_W, NA_W, 3 * D_MODEL)

M_CTX = BATCH * SEQ
M_DEC = DEC_BATCH * DEC_SEQ
M_ALL = M_CTX + M_DEC
N_COND = 1 + DEC_BATCH

LANES = 128
VMEM_LIMIT_BYTES = 56 * 1024 * 1024

GLA_A_W = 13 * LANES
GLA_BLOCK = 256
NA_QROWS = 2
NA_QBLK = NA_QROWS * GRID_W
NA_KROWS = NA_WIN_H + NA_QROWS - 1
NA_NLOC = NA_KROWS * GRID_W
DIFF_KCHUNK = 512
LOG2E = 1.4426950408889634
DIFF_QSCALE = DIFF_DK ** -0.5 * LOG2E


def _params(sem):
    return pltpu.CompilerParams(dimension_semantics=sem, vmem_limit_bytes=VMEM_LIMIT_BYTES)


def _cond_of_row(row):
    return jnp.where(row < M_CTX, 0, 1 + (row - M_CTX) // DEC_SEQ)


def _dot(a, b):
    return jnp.dot(a, b, preferred_element_type=F32)


def _dot_nt(a, b):
    return lax.dot_general(a, b, (((1,), (1,)), ((), ())), preferred_element_type=F32)


def _dot_tn(a, b):
    return lax.dot_general(a, b, (((0,), (0,)), ((), ())), preferred_element_type=F32)


def _split3(x):
    hi = x.astype(BF16)
    r1 = x - hi.astype(F32)
    mid = r1.astype(BF16)
    lo = (r1 - mid.astype(F32)).astype(BF16)
    return hi, mid, lo


def _mod_kernel(c_ref, w_ref, b_ref, o_ref):
    c = c_ref[...]
    a = (c * jax.nn.sigmoid(c)).astype(BF16)
    o_ref[0] = _dot(a, w_ref[0].astype(BF16)) + b_ref[0]


def modulation_all(cond8, w_ada, b_ada):
    n = w_ada.shape[-1]
    tn = 1024
    return pl.pallas_call(
        _mod_kernel,
        grid=(DEPTH, n // tn),
        in_specs=[pl.BlockSpec((8, D_MODEL), lambda l, j: (0, 0)),
                  pl.BlockSpec((1, D_MODEL, tn), lambda l, j: (l, 0, j)),
                  pl.BlockSpec((1, 1, tn), lambda l, j: (l, 0, j))],
        out_specs=pl.BlockSpec((1, 8, tn), lambda l, j: (l, 0, j)),
        out_shape=jax.ShapeDtypeStruct((DEPTH, 8, n), F32),
        compiler_params=_params(("arbitrary", "arbitrary")),
        name="modulation",
    )(cond8, w_ada, b_ada.reshape(DEPTH, 1, n))


def _norm_mod_kernel(xa_ref, xb_ref, g_ref, sc_ref, sh_ref, o_ref, *, n_a):
    def body(x_ref):
        x = x_ref[...]
        y = x * lax.rsqrt(jnp.mean(x * x, axis=-1, keepdims=True) + NORM_EPS) * g_ref[0]
        o_ref[...] = (y * (1.0 + sc_ref[0]) + sh_ref[0]).astype(o_ref.dtype)

    i = pl.program_id(0)

    @pl.when(i < n_a)
    def _():
        body(xa_ref)

    @pl.when(i >= n_a)
    def _():
        body(xb_ref)


def norm_mod(xs, g, layer, sc, sh, tm=512):
    n_a = M_CTX // tm
    amap, bmap = _two_source_maps(n_a, len(xs) == 1)
    xa, xb = (xs[0], xs[0]) if len(xs) == 1 else xs
    cmap = lambda i: (_cond_of_row(i * tm), 0, 0)
    return pl.pallas_call(
        functools.partial(_norm_mod_kernel, n_a=n_a),
        grid=(M_ALL // tm,),
        in_specs=[pl.BlockSpec((tm, D_MODEL), lambda i: (amap(i), 0)),
                  pl.BlockSpec((tm, D_MODEL), lambda i: (bmap(i), 0)),
                  pl.BlockSpec((1, 1, D_MODEL), lambda i: (layer, 0, 0)),
                  pl.BlockSpec((1, 1, D_MODEL), cmap),
                  pl.BlockSpec((1, 1, D_MODEL), cmap)],
        out_specs=pl.BlockSpec((tm, D_MODEL), lambda i: (i, 0)),
        out_shape=jax.ShapeDtypeStruct((M_ALL, D_MODEL), BF16),
        compiler_params=_params(("arbitrary",)),
        name="norm_mod",
    )(xa, xb, g, sc, sh)


def _rmsnorm_kernel(x_ref, g_ref, o_ref):
    x = x_ref[...]
    o_ref[...] = x * lax.rsqrt(jnp.mean(x * x, axis=-1, keepdims=True) + NORM_EPS) * g_ref[...]


def final_norm(x, g, row0, rows, tm=512):
    return pl.pallas_call(
        _rmsnorm_kernel,
        grid=(rows // tm,),
        in_specs=[pl.BlockSpec((tm, D_MODEL), lambda i: (i + row0 // tm, 0)),
                  pl.BlockSpec((1, D_MODEL), lambda i: (0, 0))],
        out_specs=pl.BlockSpec((tm, D_MODEL), lambda i: (i, 0)),
        out_shape=jax.ShapeDtypeStruct((rows, D_MODEL), F32),
        compiler_params=_params(("arbitrary",)),
        name="final_norm",
    )(x, g)


def _mm_kernel(a_ref, w_ref, o_ref, *, act):
    acc = _dot(a_ref[...], w_ref[...])
    if act == "sigmoid":
        acc = jax.nn.sigmoid(acc)
    o_ref[...] = acc.astype(o_ref.dtype)


def matmul(a, w, out_dtype, *, row0=0, rows=None, tm=1024, tn=1024, act=None, name="matmul"):
    k = a.shape[1]
    n = w.shape[1]
    rows = a.shape[0] if rows is None else rows
    tn = min(tn, n)
    assert rows % tm == 0 and row0 % tm == 0 and n % tn == 0
    r0 = row0 // tm
    return pl.pallas_call(
        functools.partial(_mm_kernel, act=act),
        grid=(rows // tm, n // tn),
        in_specs=[pl.BlockSpec((tm, k), lambda i, j: (i + r0, 0)),
                  pl.BlockSpec((k, tn), lambda i, j: (0, j))],
        out_specs=pl.BlockSpec((tm, tn), lambda i, j: (i, j)),
        out_shape=jax.ShapeDtypeStruct((rows, n), out_dtype),
        compiler_params=_params(("arbitrary", "arbitrary")),
        name=name,
    )(a, w)


def _stage_weights(i, pairs):
    @pl.when(i == 0)
    def _():
        for w_ref, w_scr in pairs:
            w_scr[...] = w_ref[0].astype(BF16)


def _merge_kernel(oa_c, ob_c, oc_c, oa_d, ob_d, oc_d, ga_ref, gb_ref, gc_ref, wa_ref, wb_ref, wc_ref, o_ref,
                  wa_scr, wb_scr, wc_scr, *, n_a):
    i = pl.program_id(1)
    _stage_weights(i, ((wa_ref, wa_scr), (wb_ref, wb_scr), (wc_ref, wc_scr)))

    def body(oa_ref, ob_ref, oc_ref):
        y = ga_ref[...].astype(F32) * _dot(oa_ref[...], wa_scr[...])
        y += gb_ref[...].astype(F32) * _dot(ob_ref[...], wb_scr[...])
        y += gc_ref[...].astype(F32) * _dot(oc_ref[...], wc_scr[...])
        o_ref[...] = y.astype(o_ref.dtype)

    @pl.when(i < n_a)
    def _():
        body(oa_c, ob_c, oc_c)

    @pl.when(i >= n_a)
    def _():
        body(oa_d, ob_d, oc_d)


def branch_merge(o_ctx, o_dec, gates, wa, wb, wc, layer, tm=512, tn=1024):
    m = M_ALL
    nb = D_MODEL // tn
    n_a = M_CTX // tm
    amap, bmap = _two_source_maps(n_a, False)
    col = lambda j, i: (layer, 0, j)
    widths = (GLA_V_W, DIFF_V_W, NA_W)
    return pl.pallas_call(
        functools.partial(_merge_kernel, n_a=n_a),
        grid=(nb, m // tm),
        in_specs=[pl.BlockSpec((tm, w), lambda j, i: (amap(i), 0)) for w in widths]
                 + [pl.BlockSpec((tm, w), lambda j, i: (bmap(i), 0)) for w in widths]
                 + [pl.BlockSpec((tm, tn), lambda j, i: (i, j)),
                  pl.BlockSpec((tm, tn), lambda j, i: (i, j + nb)),
                  pl.BlockSpec((tm, tn), lambda j, i: (i, j + 2 * nb)),
                  pl.BlockSpec((1, GLA_V_W, tn), col),
                  pl.BlockSpec((1, DIFF_V_W, tn), col),
                  pl.BlockSpec((1, NA_W, tn), col)],
        out_specs=pl.BlockSpec((tm, tn), lambda j, i: (i, j)),
        out_shape=jax.ShapeDtypeStruct((m, D_MODEL), BF16),
        scratch_shapes=[pltpu.VMEM((GLA_V_W, tn), BF16), pltpu.VMEM((DIFF_V_W, tn), BF16),
                        pltpu.VMEM((NA_W, tn), BF16)],
        compiler_params=_params(("arbitrary", "arbitrary")),
        name="branch_merge",
    )(*o_ctx, *o_dec, gates, gates, gates, wa, wb, wc)


def _two_source_maps(n_a, stacked):
    b_off = n_a if stacked else 0
    return (lambda i: jnp.minimum(i, n_a - 1)), (lambda i: jnp.maximum(i - n_a, 0) + b_off)


def _resid_mm_kernel(a_ref, w_ref, xa_ref, xb_ref, gt_ref, o_ref, w_scr, *, n_a):
    i = pl.program_id(1)
    _stage_weights(i, ((w_ref, w_scr),))
    y = gt_ref[0] * _dot(a_ref[...], w_scr[...])

    @pl.when(i < n_a)
    def _():
        o_ref[...] = xa_ref[...] + y

    @pl.when(i >= n_a)
    def _():
        o_ref[...] = xb_ref[...] + y


def resid_matmul(a, w, xs, gt, layer, tm, tn, name):
    m, k = a.shape
    n = w.shape[2]
    n_a = M_CTX // tm
    amap, bmap = _two_source_maps(n_a, len(xs) == 1)
    xa, xb = (xs[0], xs[0]) if len(xs) == 1 else xs
    return pl.pallas_call(
        functools.partial(_resid_mm_kernel, n_a=n_a),
        grid=(n // tn, m // tm),
        in_specs=[pl.BlockSpec((tm, k), lambda j, i: (i, 0)),
                  pl.BlockSpec((1, k, tn), lambda j, i: (layer, 0, j)),
                  pl.BlockSpec((tm, tn), lambda j, i: (amap(i), j)),
                  pl.BlockSpec((tm, tn), lambda j, i: (bmap(i), j)),
                  pl.BlockSpec((1, 1, tn), lambda j, i: (_cond_of_row(i * tm), 0, j))],
        out_specs=pl.BlockSpec((tm, tn), lambda j, i: (i, j)),
        out_shape=jax.ShapeDtypeStruct((m, n), F32),
        scratch_shapes=[pltpu.VMEM((k, tn), BF16)],
        compiler_params=_params(("arbitrary", "arbitrary")),
        name=name,
    )(a, w, xa, xb, gt)


def _ffn_up_kernel(h_ref, wg_ref, wu_ref, o_ref, wg_scr, wu_scr):
    _stage_weights(pl.program_id(1), ((wg_ref, wg_scr), (wu_ref, wu_scr)))
    h = h_ref[...]
    g = _dot(h, wg_scr[...])
    u = _dot(h, wu_scr[...])
    o_ref[...] = (g * jax.nn.sigmoid(g) * u).astype(o_ref.dtype)


def ffn_up(h, wg, wu, layer, tm=1024, tn=512):
    m = h.shape[0]
    wspec = pl.BlockSpec((1, D_MODEL, tn), lambda j, i: (layer, 0, j))
    return pl.pallas_call(
        _ffn_up_kernel,
        grid=(D_FF // tn, m // tm),
        in_specs=[pl.BlockSpec((tm, D_MODEL), lambda j, i: (i, 0)), wspec, wspec],
        out_specs=pl.BlockSpec((tm, tn), lambda j, i: (i, j)),
        out_shape=jax.ShapeDtypeStruct((m, D_FF), BF16),
        scratch_shapes=[pltpu.VMEM((D_MODEL, tn), BF16), pltpu.VMEM((D_MODEL, tn), BF16)],
        compiler_params=_params(("arbitrary", "arbitrary")),
        name="ffn_up",
    )(h, wg, wu)


def _gla_kernel(q_ref, k_ref, v_ref, gr_ref, lr_ref, wd_ref, bd_ref, g_ref, *rest, nblk, has_state):
    if has_state:
        s0f_ref, s0b_ref, o_ref, sfin_ref, s_scr, of_scr = rest
    else:
        o_ref, sfin_ref, s_scr, of_scr = rest
    pss = pl.program_id(1)
    blk = pl.program_id(2)
    C = GLA_CHUNK
    nch = GLA_BLOCK // C

    if has_state:
        @pl.when(jnp.logical_and(blk == 0, pss == 0))
        def _init_f():
            s_scr[...] = s0f_ref[0, 0]

        @pl.when(jnp.logical_and(blk == 0, pss == 1))
        def _init_b():
            s_scr[...] = s0b_ref[0, 0]
    else:
        @pl.when(blk == 0)
        def _init():
            s_scr[...] = jnp.zeros_like(s_scr)

    lr = lr_ref[...]
    lr_hi = lr.astype(BF16)
    lr_lo = (lr - lr_hi.astype(F32)).astype(BF16)
    wd = wd_ref[0]
    wd_hi = wd.astype(BF16)
    wd_lo = (wd - wd_hi.astype(F32)).astype(BF16)
    pre = _dot(lr_hi, wd_hi) + _dot(lr_hi, wd_lo) + _dot(lr_lo, wd_hi) + bd_ref[0]
    la_all = jax.nn.log_sigmoid(pre) * (1.0 / GLA_TAU)

    row = lax.broadcasted_iota(jnp.int32, (C, C), 0)
    colm = lax.broadcasted_iota(jnp.int32, (C, C), 1)
    lane = lax.broadcasted_iota(jnp.int32, (1, LANES), 1)
    head_mask = (lane < GLA_DK, lane >= GLA_DK)
    eye = (lax.broadcasted_iota(jnp.int32, (LANES, LANES), 0)
           == lax.broadcasted_iota(jnp.int32, (LANES, LANES), 1))

    def run_direction(backward):
        keep = (colm >= row) if backward else (colm <= row)
        tri = jnp.where(keep, 1.0, 0.0).astype(BF16)
        last, mid = (0, C // 2) if backward else (C - 1, C // 2 - 1)
        order = range(nch - 1, -1, -1) if backward else range(nch)
        for c in order:
            rs = slice(c * C, (c + 1) * C)
            for p in range(GLA_HEADS // 2):
                ls = slice(p * LANES, (p + 1) * LANES)
                la = la_all[rs, ls]
                hi, md, lo = _split3(la)
                b = _dot(tri, hi) + _dot(tri, md) + _dot(tri, lo)
                b_last = b[last:last + 1, :]
                b_mid = b[mid:mid + 1, :]
                q = q_ref[rs, ls] * (GLA_DK ** -0.5)
                k = k_ref[rs, ls]
                q_in = q * jnp.exp(b - b_mid)
                k_in = (k * jnp.exp(b_mid - b)).astype(BF16)
                q_ex = q * jnp.exp(b)
                k_dc = k * jnp.exp(b_last - b)
                dec_col = jnp.exp(jnp.sum(jnp.where(eye, b_last, 0.0), axis=1, keepdims=True))
                s_old = s_scr[p]
                s_old_bf = s_old.astype(BF16)
                s_new = dec_col * s_old
                for h in range(2):
                    hd = 2 * p + h
                    vs = slice(hd * GLA_DV, (hd + 1) * GLA_DV)
                    v = v_ref[rs, vs].astype(BF16)
                    att = _dot_nt(jnp.where(head_mask[h], q_in, 0.0).astype(BF16), k_in)
                    att = jnp.where(keep, att, 0.0).astype(BF16)
                    o = _dot(att, v) + _dot(jnp.where(head_mask[h], q_ex, 0.0).astype(BF16), s_old_bf)
                    s_new = s_new + _dot_tn(jnp.where(head_mask[h], k_dc, 0.0).astype(BF16), v)
                    if backward:
                        tot = of_scr[blk_rows(c), vs] + o
                        y = tot * lax.rsqrt(jnp.mean(tot * tot, axis=-1, keepdims=True) + NORM_EPS) * g_ref[0]
                        gr = gr_ref[rs, vs]
                        o_ref[rs, vs] = (y * (gr * jax.nn.sigmoid(gr))).astype(o_ref.dtype)
                    else:
                        of_scr[blk_rows(c), vs] = o
                s_scr[p] = s_new

    def blk_rows(c):
        pos = jnp.where(pss == 0, blk, nblk - 1 - blk)
        return pl.ds(pl.multiple_of(pos * GLA_BLOCK + c * C, C), C)

    @pl.when(pss == 0)
    def _fwd():
        run_direction(False)

    @pl.when(pss == 1)
    def _bwd():
        run_direction(True)

    @pl.when(blk == nblk - 1)
    def _fin():
        sfin_ref[0, 0] = s_scr[...]


def gla_mixer(pa, wd, bd, g, layer, *, row0, nseq, seqlen, state=None):
    nblk = seqlen // GLA_BLOCK
    b0 = row0 // GLA_BLOCK
    has_state = state is not None

    def rb(s, p, b):
        return b0 + s * nblk + b + p * (nblk - 1 - 2 * b)

    def orb(s, p, b):
        return s * nblk + nblk - 1 - p * b

    in_specs = [pl.BlockSpec((GLA_BLOCK, GLA_QK_W), lambda s, p, b: (rb(s, p, b), 0)),
                pl.BlockSpec((GLA_BLOCK, GLA_QK_W), lambda s, p, b: (rb(s, p, b), 1)),
                pl.BlockSpec((GLA_BLOCK, GLA_V_W), lambda s, p, b: (rb(s, p, b), 1)),
                pl.BlockSpec((GLA_BLOCK, GLA_V_W), lambda s, p, b: (rb(s, p, b), 2)),
                pl.BlockSpec((GLA_BLOCK, LANES), lambda s, p, b: (rb(s, p, b), GLA_A_W // LANES - 1)),
                pl.BlockSpec((1, LANES, GLA_QK_W), lambda s, p, b: (p, 0, 0)),
                pl.BlockSpec((1, 1, GLA_QK_W), lambda s, p, b: (p, 0, 0)),
                pl.BlockSpec((1, 1, GLA_DV), lambda s, p, b: (layer, 0, 0))]
    args = [pa, pa, pa, pa, pa, wd, bd, g]
    if has_state:
        st_spec = pl.BlockSpec((1, 1, 2, LANES, LANES), lambda s, p, b: (s, layer, 0, 0, 0))
        in_specs += [st_spec, st_spec]
        args += [state[0], state[1]]
    kern = functools.partial(_gla_kernel, nblk=nblk, has_state=has_state)
    return pl.pallas_call(
        kern,
        grid=(nseq, 2, nblk),
        in_specs=in_specs,
        out_specs=[pl.BlockSpec((GLA_BLOCK, GLA_V_W), lambda s, p, b: (orb(s, p, b), 0)),
                   pl.BlockSpec((1, 1, 2, LANES, LANES), lambda s, p, b: (p, s, 0, 0, 0))],
        out_shape=[jax.ShapeDtypeStruct((nseq * seqlen, GLA_V_W), BF16),
                   jax.ShapeDtypeStruct((2, nseq, 2, LANES, LANES), F32)],
        scratch_shapes=[pltpu.VMEM((2, LANES, LANES), F32),
                        pltpu.VMEM((seqlen, GLA_V_W), F32)],
        compiler_params=_params(("arbitrary", "arbitrary", "arbitrary")),
        name="gla_mixer",
    )(*args)


def _rope_kernel(x_ref, cos_ref, sin_ref, o_ref):
    cos = cos_ref[...]
    sin = sin_ref[...]
    lane = lax.broadcasted_iota(jnp.int32, (1, LANES), 1)
    first = (lane % (2 * ROPE_NFREQ)) < ROPE_NFREQ
    nq = DIFF_QK_W // LANES
    for j in range(2 * nq):
        x = x_ref[:, j * LANES:(j + 1) * LANES].astype(F32)
        partner = jnp.where(first, pltpu.roll(x, LANES - ROPE_NFREQ, 1), pltpu.roll(x, ROPE_NFREQ, 1))
        y = x * cos + partner * sin
        if j < nq:
            y = y * DIFF_QSCALE
        o_ref[:, j * LANES:(j + 1) * LANES] = y.astype(o_ref.dtype)


def rope_qk(pb_dec, cos_t, sin_t, tm=512):
    nt = DEC_SEQ // tm
    return pl.pallas_call(
        _rope_kernel,
        grid=(M_DEC // tm,),
        in_specs=[pl.BlockSpec((tm, 2 * DIFF_QK_W), lambda i: (i, 0)),
                  pl.BlockSpec((tm, LANES), lambda i: (i % nt, 0)),
                  pl.BlockSpec((tm, LANES), lambda i: (i % nt, 0))],
        out_specs=pl.BlockSpec((tm, 2 * DIFF_QK_W), lambda i: (i, 0)),
        out_shape=jax.ShapeDtypeStruct((M_DEC, 2 * DIFF_QK_W), BF16),
        compiler_params=_params(("arbitrary",)),
        name="rope_qk",
    )(pb_dec, cos_t, sin_t)


def _diff_head(q, chunks, lam, g, lam_init):
    lane = lax.broadcasted_iota(jnp.int32, (1, LANES), 1)
    qm = [jnp.where(lane < DIFF_DK, q, jnp.zeros_like(q)), jnp.where(lane >= DIFF_DK, q, jnp.zeros_like(q))]
    m = [None, None]
    acc = [None, None]
    for k, v in chunks:
        for mp in range(2):
            s = _dot_nt(qm[mp], k)
            cm = jnp.max(s, axis=-1, keepdims=True)
            if m[mp] is None:
                m[mp] = cm
                acc[mp] = _dot(jnp.exp2(s - cm).astype(BF16), v)
            else:
                m_new = jnp.maximum(m[mp], cm)
                acc[mp] = jnp.exp2(m[mp] - m_new) * acc[mp] + _dot(jnp.exp2(s - m_new).astype(BF16), v)
                m[mp] = m_new
    outs = [a[:, :LANES] / a[:, LANES:] for a in acc]
    o = outs[0] - lam * outs[1]
    y = o * lax.rsqrt(jnp.mean(o * o, axis=-1, keepdims=True) + NORM_EPS) * g
    return y * (1.0 - lam_init)


def _diff_kernel(lam_ref, q_ref, kn_ref, vn_ref, kc_ref, vc_ref, g_ref, o_ref, vaug, kc_scr, *, lam_init):
    n_new = kn_ref.shape[0]
    n_c = PAST_LEN

    @pl.when(pl.program_id(2) == 0)
    def _stage():
        vaug[:, LANES:] = jnp.ones((n_c + n_new, LANES), BF16)
        vaug[n_c:, :LANES] = vn_ref[...]
        vaug[:n_c, :LANES] = vc_ref[0, 0].astype(BF16)
        kc_scr[...] = kc_ref[0, 0].astype(BF16)

    chunks = [(kc_scr[...], vaug[:n_c, :])]
    for c in range(n_new // DIFF_KCHUNK):
        rows = slice(c * DIFF_KCHUNK, (c + 1) * DIFF_KCHUNK)
        chunks.append((kn_ref[rows, :], vaug[n_c + c * DIFF_KCHUNK:n_c + (c + 1) * DIFF_KCHUNK, :]))
    y = _diff_head(q_ref[...], chunks, lam_ref[0], g_ref[0], lam_init)
    o_ref[...] = y.astype(o_ref.dtype)


def _diff_ctx_kernel(lam_ref, q_ref, k_ref, v_ref, g_ref, o_ref, *, lam_init):
    ones = jnp.ones((SEQ, LANES), BF16)
    for h in range(DIFF_HEADS):
        ls = slice(h * LANES, (h + 1) * LANES)
        q = (q_ref[:, ls] * DIFF_QSCALE).astype(BF16)
        vaug = jnp.concatenate([v_ref[:, ls].astype(BF16), ones], axis=1)
        y = _diff_head(q, [(k_ref[:, ls].astype(BF16), vaug)], lam_ref[0], g_ref[0], lam_init)
        o_ref[:, ls] = y.astype(o_ref.dtype)


def diff_attention_ctx(pb_ctx, lam, g, layer, lam_init):
    return pl.pallas_call(
        functools.partial(_diff_ctx_kernel, lam_init=lam_init),
        grid=(BATCH,),
        in_specs=[pl.BlockSpec(memory_space=pltpu.SMEM),
                  pl.BlockSpec((SEQ, DIFF_QK_W), lambda s: (s, 0)),
                  pl.BlockSpec((SEQ, DIFF_QK_W), lambda s: (s, 1)),
                  pl.BlockSpec((SEQ, DIFF_V_W), lambda s: (s, 2)),
                  pl.BlockSpec((1, 1, DIFF_DV), lambda s: (layer, 0, 0))],
        out_specs=pl.BlockSpec((SEQ, DIFF_V_W), lambda s: (s, 0)),
        out_shape=jax.ShapeDtypeStruct((M_CTX, DIFF_V_W), BF16),
        compiler_params=_params(("arbitrary",)),
        name="diff_attention_ctx",
    )(lam, pb_ctx, pb_ctx, pb_ctx, g)


def diff_attention_dec(qk_rope, pb_dec, cache_k, cache_v, layer, lam, g, lam_init, tq=512):
    kern = functools.partial(_diff_kernel, lam_init=lam_init)
    H = DIFF_HEADS
    nqb = DEC_SEQ // tq
    ntot = PAST_LEN + DEC_SEQ
    return pl.pallas_call(
        kern,
        grid=(DEC_BATCH, H, nqb),
        in_specs=[pl.BlockSpec(memory_space=pltpu.SMEM),
                  pl.BlockSpec((tq, LANES), lambda s, h, b: (s * nqb + b, h)),
                  pl.BlockSpec((DEC_SEQ, LANES), lambda s, h, b: (s, H + h)),
                  pl.BlockSpec((DEC_SEQ, LANES), lambda s, h, b: (s, 2 * H + h)),
                  pl.BlockSpec((1, 1, PAST_LEN, LANES), lambda s, h, b: (s, layer, 0, h)),
                  pl.BlockSpec((1, 1, PAST_LEN, LANES), lambda s, h, b: (s, layer, 0, h)),
                  pl.BlockSpec((1, 1, DIFF_DV), lambda s, h, b: (layer, 0, 0))],
        out_specs=pl.BlockSpec((tq, LANES), lambda s, h, b: (s * nqb + b, h)),
        out_shape=jax.ShapeDtypeStruct((M_DEC, DIFF_V_W), BF16),
        scratch_shapes=[pltpu.VMEM((ntot, 2 * LANES), BF16),
                        pltpu.VMEM((PAST_LEN, LANES), BF16)],
        compiler_params=_params(("arbitrary", "arbitrary", "arbitrary")),
        name="diff_attention_dec",
    )(lam, qk_rope, qk_rope, pb_dec, cache_k, cache_v, g)


def _softmax_pair(q, keys_vals, biases):
    lane = lax.broadcasted_iota(jnp.int32, (1, LANES), 1)
    outs = []
    for h in range(2):
        msk = (lane < NA_DK) if h == 0 else (lane >= NA_DK)
        qm = jnp.where(msk, q, jnp.zeros_like(q))
        ss = []
        for (k, _), bias in zip(keys_vals, biases):
            s = _dot_nt(qm, k)
            if bias is not None:
                s = s + bias(h)
            ss.append(s)
        m = functools.reduce(jnp.maximum, [jnp.max(s, axis=-1, keepdims=True) for s in ss])
        num = 0.0
        den = 0.0
        for s, (_, v) in zip(ss, keys_vals):
            e = jnp.exp(s - m)
            den = den + jnp.sum(e, axis=-1, keepdims=True)
            num = num + _dot(e.astype(BF16), v)
        outs.append(num / den)
    return jnp.where(lane < NA_DK, outs[0], outs[1])


def _ctx_attn_kernel(q_ref, k_ref, v_ref, o_ref):
    for p in range(NA_HEADS // 2):
        ls = slice(p * LANES, (p + 1) * LANES)
        q = (q_ref[:, ls] * (NA_DK ** -0.5)).astype(BF16)
        k = k_ref[:, ls].astype(BF16)
        v = v_ref[:, ls].astype(BF16)
        o_ref[:, ls] = _softmax_pair(q, [(k, v)], [None]).astype(o_ref.dtype)


def context_attention_ctx(pc_ctx):
    return pl.pallas_call(
        _ctx_attn_kernel,
        grid=(BATCH,),
        in_specs=[pl.BlockSpec((SEQ, NA_W), lambda s: (s, 0)),
                  pl.BlockSpec((SEQ, NA_W), lambda s: (s, 1)),
                  pl.BlockSpec((SEQ, NA_W), lambda s: (s, 2))],
        out_specs=pl.BlockSpec((SEQ, NA_W), lambda s: (s, 0)),
        out_shape=jax.ShapeDtypeStruct((M_CTX, NA_W), BF16),
        compiler_params=_params(("arbitrary",)),
        name="context_attention",
    )(pc_ctx, pc_ctx, pc_ctx)


def _na_window_start(rb):
    return jnp.clip(NA_QROWS * rb - NA_WIN_H // 2, 0, GRID_H - NA_KROWS)


def _na_kernel(q_ref, k_ref, v_ref, kc_ref, vc_ref, bias_ref, o_ref, kc_scr, vc_scr):
    rb = pl.program_id(1)

    @pl.when(rb == 0)
    def _stage():
        kc_scr[...] = kc_ref[0, 0].astype(BF16)
        vc_scr[...] = vc_ref[0, 0].astype(BF16)

    start = pl.multiple_of(_na_window_start(rb) * GRID_W, GRID_W)
    for p in range(NA_HEADS // 2):
        ls = slice(p * LANES, (p + 1) * LANES)
        kl = k_ref[pl.ds(start, NA_NLOC), ls]
        vl = v_ref[pl.ds(start, NA_NLOC), ls]
        q = q_ref[:, ls] * (NA_DK ** -0.5)
        o = _softmax_pair(q, [(kl, vl), (kc_scr[:, ls], vc_scr[:, ls])],
                          [lambda h, p=p: bias_ref[0, 2 * p + h], None])
        o_ref[:, ls] = o.astype(o_ref.dtype)


def _na_situation(rb):
    nrb = GRID_H // NA_QROWS
    return jnp.where(rb < 2, rb, jnp.where(rb >= nrb - 2, rb - (nrb - 5), 2))


def neighbourhood_attention_dec(pc_dec, cache_k, cache_v, layer, bias):
    nrb = GRID_H // NA_QROWS
    cspec = pl.BlockSpec((1, 1, PAST_LEN, NA_W), lambda s, r: (s, layer, 0, 0))
    return pl.pallas_call(
        _na_kernel,
        grid=(DEC_BATCH, nrb),
        in_specs=[pl.BlockSpec((NA_QBLK, NA_W), lambda s, r: (s * nrb + r, 0)),
                  pl.BlockSpec((DEC_SEQ, NA_W), lambda s, r: (s, 1)),
                  pl.BlockSpec((DEC_SEQ, NA_W), lambda s, r: (s, 2)),
                  cspec, cspec,
                  pl.BlockSpec((1, NA_HEADS, NA_QBLK, NA_NLOC), lambda s, r: (_na_situation(r), 0, 0, 0))],
        out_specs=pl.BlockSpec((NA_QBLK, NA_W), lambda s, r: (s * nrb + r, 0)),
        out_shape=jax.ShapeDtypeStruct((M_DEC, NA_W), BF16),
        scratch_shapes=[pltpu.VMEM((PAST_LEN, NA_W), BF16), pltpu.VMEM((PAST_LEN, NA_W), BF16)],
        compiler_params=_params(("arbitrary", "arbitrary")),
        name="neighbourhood_attention",
    )(pc_dec, pc_dec, pc_dec, cache_k, cache_v, bias)


def _na_bias_tables(rpb):
    nrb = GRID_H // NA_QROWS
    rbs = [0, 1, 2, nrb - 2, nrb - 1]
    qc = np.arange(GRID_W)
    kc = np.arange(GRID_W)
    col_start = np.clip(qc - NA_WIN_W // 2, 0, GRID_W - NA_WIN_W)
    col_ok = (kc[None, :] >= col_start[:, None]) & (kc[None, :] < col_start[:, None] + NA_WIN_W)
    dcol = np.clip(kc[None, :] - qc[:, None], 1 - NA_WIN_W, NA_WIN_W - 1) + NA_WIN_W - 1
    drow_all, ok_all = [], []
    for rb in rbs:
        ws = int(np.clip(NA_QROWS * rb - NA_WIN_H // 2, 0, GRID_H - NA_KROWS))
        qr = NA_QROWS * rb + np.arange(NA_QROWS)
        kr = ws + np.arange(NA_KROWS)
        win = np.clip(qr - NA_WIN_H // 2, 0, GRID_H - NA_WIN_H)
        row_ok = (kr[None, :] >= win[:, None]) & (kr[None, :] < win[:, None] + NA_WIN_H)
        drow_all.append(np.clip(kr[None, :] - qr[:, None] + NA_WIN_H - 1, 0, 2 * NA_WIN_H - 2))
        ok_all.append(row_ok[:, None, :, None] & col_ok[None, :, None, :])
    nd, nc = 2 * NA_WIN_H - 1, 2 * NA_WIN_W - 1
    sel_row = (np.stack(drow_all).reshape(-1)[:, None] == np.arange(nd)[None, :]).astype(np.float32)
    sel_col = (np.arange(nc)[:, None] == dcol.reshape(-1)[None, :]).astype(np.float32)
    tab = jnp.einsum("nd,hdc,cx->hnx", sel_row, rpb.astype(F32), sel_col, precision=lax.Precision.HIGHEST)
    tab = tab.reshape(NA_HEADS, 5, NA_QROWS, NA_KROWS, GRID_W, GRID_W)
    tab = jnp.transpose(tab, (1, 0, 2, 4, 3, 5))
    ok = np.stack(ok_all)[:, None]
    return jnp.where(ok, tab, NEG_INF).reshape(5, NA_HEADS, NA_QBLK, NA_NLOC)


def _rope_tables():
    pos = np.arange(DEC_SEQ)
    row = (pos // GRID_W).astype(np.float32)
    col = (pos % GRID_W).astype(np.float32)
    inv = jnp.asarray(ROPE_BASE, F32) ** (-jnp.arange(ROPE_NFREQ, dtype=F32) / ROPE_NFREQ)
    ang_r = jnp.asarray(row)[:, None] * inv
    ang_c = jnp.asarray(col)[:, None] * inv
    cos64 = jnp.concatenate([jnp.cos(ang_r), jnp.cos(ang_r), jnp.cos(ang_c), jnp.cos(ang_c)], axis=1)
    sin64 = jnp.concatenate([-jnp.sin(ang_r), jnp.sin(ang_r), -jnp.sin(ang_c), jnp.sin(ang_c)], axis=1)
    return jnp.tile(cos64, (1, 2)), jnp.tile(sin64, (1, 2))


def kernel(x_prompt, x_sample, c, cache_diff_k, cache_diff_v, cache_na_k, cache_na_v, state_gla_fwd,
           state_gla_bwd, c_ctx, w_ada, b_ada, norm_mix_g, w_in, w_decay, b_decay, gla_norm_g, diff_lambda,
           diff_norm_g, na_rpb, w_br_a, w_br_b, w_br_c, w_out, norm_ffn_g, w_ffn_gate, w_ffn_up, w_ffn_down,
           norm_final_g):
    D = D_MODEL
    xs = (x_prompt.reshape(M_CTX, D), x_sample.reshape(M_DEC, D))
    cond8 = jnp.concatenate([c_ctx[None], c, jnp.zeros((8 - N_COND, D), F32)], axis=0)
    mod = modulation_all(cond8, w_ada, b_ada)
    cos_t, sin_t = _rope_tables()
    g_mix = norm_mix_g.reshape(DEPTH, 1, D)
    g_ffn = norm_ffn_g.reshape(DEPTH, 1, D)
    g_gla = gla_norm_g.reshape(DEPTH, 1, GLA_DV)
    g_diff = diff_norm_g.reshape(DEPTH, 1, DIFF_DV)
    ck_diff = cache_diff_k.reshape(DEC_BATCH, DEPTH, PAST_LEN, DIFF_QK_W)
    cv_diff = cache_diff_v.reshape(DEC_BATCH, DEPTH, PAST_LEN, DIFF_V_W)
    ck_na = cache_na_k.reshape(DEC_BATCH, DEPTH, PAST_LEN, NA_W)
    cv_na = cache_na_v.reshape(DEC_BATCH, DEPTH, PAST_LEN, NA_W)
    gla_state = (state_gla_fwd.reshape(DEC_BATCH, DEPTH, 2, LANES, LANES),
                 state_gla_bwd.reshape(DEC_BATCH, DEPTH, 2, LANES, LANES))

    offs = np.cumsum((0,) + IN_WIDTHS)
    a_end, b_end, c_end = offs[5], offs[8], offs[11]
    new_state = []
    for l in range(DEPTH):
        lam_init = 0.8 - 0.6 * float(np.exp(-0.3 * l))
        m6 = mod[l, :N_COND].reshape(N_COND, 6, 1, D)
        sh_a, sc_a, gt_a, sh_f, sc_f, gt_f = (m6[:, i] for i in range(6))
        wl = w_in[l]
        w_a = jnp.concatenate([wl[:, :a_end], jnp.zeros((D, GLA_A_W - a_end), F32)], axis=1).astype(BF16)
        w_b = wl[:, a_end:b_end].astype(BF16)
        w_c = wl[:, b_end:c_end].astype(BF16)
        w_g = wl[:, c_end:].astype(BF16)

        h = norm_mod(xs, g_mix, l, sc_a, sh_a)
        pa = matmul(h, w_a, F32, tm=512, tn=GLA_A_W, name="in_proj_gla")
        pb_ctx = matmul(h, w_b, F32, rows=M_CTX, name="in_proj_diff_ctx")
        pb_dec = matmul(h, w_b, BF16, row0=M_CTX, rows=M_DEC, name="in_proj_diff_dec")
        pc_ctx = matmul(h, w_c, F32, rows=M_CTX, tn=768, name="in_proj_na_ctx")
        pc_dec = matmul(h, w_c, BF16, row0=M_CTX, rows=M_DEC, tn=768, name="in_proj_na_dec")
        gates = matmul(h, w_g, BF16, act="sigmoid", name="in_proj_gates")

        wd = jnp.zeros((2, LANES, GLA_QK_W), F32)
        wd = wd.at[0, :GLA_LR].set(w_decay[l, 0]).at[1, GLA_LR:2 * GLA_LR].set(w_decay[l, 1])
        bd = b_decay[l][:, None, :]
        oa_c, s_fin = gla_mixer(pa, wd, bd, g_gla, l, row0=0, nseq=BATCH, seqlen=SEQ)
        oa_d, _ = gla_mixer(pa, wd, bd, g_gla, l, row0=M_CTX, nseq=DEC_BATCH, seqlen=DEC_SEQ, state=gla_state)

        lp = diff_lambda[l].astype(F32)
        lam = (jnp.exp(jnp.sum(lp[0] * lp[1])) - jnp.exp(jnp.sum(lp[2] * lp[3])) + lam_init).reshape(1)
        ob_c = diff_attention_ctx(pb_ctx, lam, g_diff, l, lam_init)
        qk_rope = rope_qk(pb_dec, cos_t, sin_t)
        ob_d = diff_attention_dec(qk_rope, pb_dec, ck_diff, cv_diff, l, lam, g_diff, lam_init)

        oc_c = context_attention_ctx(pc_ctx)
        oc_d = neighbourhood_attention_dec(pc_dec, ck_na, cv_na, l, _na_bias_tables(na_rpb[l]))

        merged = branch_merge((oa_c, ob_c, oc_c), (oa_d, ob_d, oc_d), gates, w_br_a, w_br_b, w_br_c, l)
        x = resid_matmul(merged, w_out, xs, gt_a, l, tm=512, tn=1024, name="out_proj")
        xs = (x,)

        h2 = norm_mod(xs, g_ffn, l, sc_f, sh_f)
        act = ffn_up(h2, w_ffn_gate, w_ffn_up, l)
        x = resid_matmul(act, w_ffn_down, xs, gt_f, l, tm=512, tn=512, name="ffn_down")
        xs = (x,)

        new_state.append((
            pb_ctx[:, DIFF_QK_W:2 * DIFF_QK_W].reshape(BATCH, SEQ, DIFF_HEADS, 2 * DIFF_DK),
            pb_ctx[:, 2 * DIFF_QK_W:].reshape(BATCH, SEQ, DIFF_HEADS, DIFF_DV),
            pc_ctx[:, NA_W:2 * NA_W].reshape(BATCH, SEQ, NA_HEADS, NA_DK),
            pc_ctx[:, 2 * NA_W:].reshape(BATCH, SEQ, NA_HEADS, NA_DK),
            s_fin[0].reshape(BATCH, GLA_HEADS, GLA_DK, GLA_DV),
            s_fin[1].reshape(BATCH, GLA_HEADS, GLA_DK, GLA_DV)))

    gfin = norm_final_g[None]
    y_prompt = final_norm(x, gfin, 0, M_CTX).reshape(BATCH, SEQ, D)
    y_sample = final_norm(x, gfin, M_CTX, M_DEC).reshape(DEC_BATCH, DEC_SEQ, D)
    stacked = tuple(jnp.stack([st[i] for st in new_state], axis=1) for i in range(6))
    return (y_prompt, y_sample) + stacked
```

```python
import functools

import numpy as np
import jax
import jax.numpy as jnp
from jax import lax
from jax.experimental import pallas as pl
from jax.experimental.pallas import tpu as pltpu

F32 = jnp.float32
BF16 = jnp.bfloat16

D_MODEL = 2048
BATCH = 16
SEQ = 256
DEPTH = 2
DEC_BATCH = 2
DEC_SEQ = 4096
PAST_LEN = 512
GRID_W = 64
GRID_H = DEC_SEQ // GRID_W
NORM_EPS = 1e-6
NEG_INF = -1e30
GLA_HEADS = 4
GLA_DK = 64
GLA_DV = 128
GLA_LR = 16
GLA_TAU = 16.0
GLA_CHUNK = 64
GLA_QK_W = GLA_HEADS * GLA_DK
GLA_V_W = GLA_HEADS * GLA_DV
DIFF_HEADS = 8
DIFF_DK = 64
DIFF_DV = 128
DIFF_QK_W = DIFF_HEADS * 2 * DIFF_DK
DIFF_V_W = DIFF_HEADS * DIFF_DV
ROPE_BASE = 10000.0
ROPE_NFREQ = DIFF_DK // 4
NA_HEADS = 8
NA_DK = 64
NA_W = NA_HEADS * NA_DK
NA_WIN_H = 8
NA_WIN_W = 16
D_FF = ((8 * D_MODEL // 3 + 255) // 256) * 256
IN_WIDTHS = (GLA_QK_W, GLA_QK_W, GLA_V_W, GLA_V_W, 2 * GLA_LR,
             DIFF_QK_W, DIFF_QK_W, DIFF_V_W, NA_W, NA_W, NA_W, 3 * D_MODEL)

M_CTX = BATCH * SEQ
M_DEC = DEC_BATCH * DEC_SEQ
M_ALL = M_CTX + M_DEC
N_COND = 1 + DEC_BATCH

LANES = 128
VMEM_LIMIT_BYTES = 56 * 1024 * 1024

GLA_A_W = 13 * LANES
IN_B0, IN_C0, IN_G0 = 2048, 5120, 7168
GLA_BLOCK = 256
NA_QROWS = 2
NA_QBLK = NA_QROWS * GRID_W
NA_KROWS = NA_WIN_H + NA_QROWS - 1
NA_NLOC = NA_KROWS * GRID_W
DIFF_KCHUNK = 512
LOG2E = 1.4426950408889634
DIFF_QSCALE = DIFF_DK ** -0.5 * LOG2E


def _params(sem):
    return pltpu.CompilerParams(dimension_semantics=sem, vmem_limit_bytes=VMEM_LIMIT_BYTES)


def _cond_of_row(row):
    return jnp.where(row < M_CTX, 0, 1 + (row - M_CTX) // DEC_SEQ)


def _dot(a, b):
    return jnp.dot(a, b, preferred_element_type=F32)


def _dot_nt(a, b):
    return lax.dot_general(a, b, (((1,), (1,)), ((), ())), preferred_element_type=F32)


def _dot_tn(a, b):
    return lax.dot_general(a, b, (((0,), (0,)), ((), ())), preferred_element_type=F32)


def _split3(x):
    hi = x.astype(BF16)
    r1 = x - hi.astype(F32)
    mid = r1.astype(BF16)
    lo = (r1 - mid.astype(F32)).astype(BF16)
    return hi, mid, lo


def _mod_kernel(c_ref, w_ref, b_ref, o_ref):
    c = c_ref[...]
    a = (c * jax.nn.sigmoid(c)).astype(BF16)
    o_ref[0] = _dot(a, w_ref[0].astype(BF16)) + b_ref[0]


def modulation_all(cond8, w_ada, b_ada):
    n = w_ada.shape[-1]
    tn = 1024
    return pl.pallas_call(
        _mod_kernel,
        grid=(DEPTH, n // tn),
        in_specs=[pl.BlockSpec((8, D_MODEL), lambda l, j: (0, 0)),
                  pl.BlockSpec((1, D_MODEL, tn), lambda l, j: (l, 0, j)),
                  pl.BlockSpec((1, 1, tn), lambda l, j: (l, 0, j))],
        out_specs=pl.BlockSpec((1, 8, tn), lambda l, j: (l, 0, j)),
        out_shape=jax.ShapeDtypeStruct((DEPTH, 8, n), F32),
        compiler_params=_params(("arbitrary", "arbitrary")),
        name="modulation",
    )(cond8, w_ada, b_ada.reshape(DEPTH, 1, n))


def _norm_mod_kernel(xa_ref, xb_ref, g_ref, sc_ref, sh_ref, o_ref, *, n_a):
    def body(x_ref):
        x = x_ref[...]
        y = x * lax.rsqrt(jnp.mean(x * x, axis=-1, keepdims=True) + NORM_EPS) * g_ref[0]
        o_ref[...] = (y * (1.0 + sc_ref[0]) + sh_ref[0]).astype(o_ref.dtype)

    i = pl.program_id(0)

    @pl.when(i < n_a)
    def _():
        body(xa_ref)

    @pl.when(i >= n_a)
    def _():
        body(xb_ref)


def norm_mod(xs, g, layer, sc, sh, tm=512):
    n_a = M_CTX // tm
    amap, bmap = _two_source_maps(n_a, len(xs) == 1)
    xa, xb = (xs[0], xs[0]) if len(xs) == 1 else xs
    cmap = lambda i: (_cond_of_row(i * tm), 0, 0)
    return pl.pallas_call(
        functools.partial(_norm_mod_kernel, n_a=n_a),
        grid=(M_ALL // tm,),
        in_specs=[pl.BlockSpec((tm, D_MODEL), lambda i: (amap(i), 0)),
                  pl.BlockSpec((tm, D_MODEL), lambda i: (bmap(i), 0)),
                  pl.BlockSpec((1, 1, D_MODEL), lambda i: (layer, 0, 0)),
                  pl.BlockSpec((1, 1, D_MODEL), cmap),
                  pl.BlockSpec((1, 1, D_MODEL), cmap)],
        out_specs=pl.BlockSpec((tm, D_MODEL), lambda i: (i, 0)),
        out_shape=jax.ShapeDtypeStruct((M_ALL, D_MODEL), BF16),
        compiler_params=_params(("arbitrary",)),
        name="norm_mod",
    )(xa, xb, g, sc, sh)


def _rmsnorm_kernel(x_ref, g_ref, o_ref):
    x = x_ref[...]
    o_ref[...] = x * lax.rsqrt(jnp.mean(x * x, axis=-1, keepdims=True) + NORM_EPS) * g_ref[...]


def final_norm(x, g, row0, rows, tm=512):
    return pl.pallas_call(
        _rmsnorm_kernel,
        grid=(rows // tm,),
        in_specs=[pl.BlockSpec((tm, D_MODEL), lambda i: (i + row0 // tm, 0)),
                  pl.BlockSpec((1, D_MODEL), lambda i: (0, 0))],
        out_specs=pl.BlockSpec((tm, D_MODEL), lambda i: (i, 0)),
        out_shape=jax.ShapeDtypeStruct((rows, D_MODEL), F32),
        compiler_params=_params(("arbitrary",)),
        name="final_norm",
    )(x, g)


def _mm_kernel(a_ref, w_ref, o_ref, *, act):
    acc = _dot(a_ref[...], w_ref[...])
    if act == "sigmoid":
        acc = jax.nn.sigmoid(acc)
    o_ref[...] = acc.astype(o_ref.dtype)


def matmul(a, w, out_dtype, *, col0, n, row0=0, rows=None, tm=1024, tn=1024, act=None, name="matmul"):
    k = a.shape[1]
    rows = a.shape[0] if rows is None else rows
    assert rows % tm == 0 and row0 % tm == 0 and n % tn == 0 and col0 % tn == 0
    r0 = row0 // tm
    c0 = col0 // tn
    return pl.pallas_call(
        functools.partial(_mm_kernel, act=act),
        grid=(rows // tm, n // tn),
        in_specs=[pl.BlockSpec((tm, k), lambda i, j: (i + r0, 0)),
                  pl.BlockSpec((k, tn), lambda i, j: (0, j + c0))],
        out_specs=pl.BlockSpec((tm, tn), lambda i, j: (i, j)),
        out_shape=jax.ShapeDtypeStruct((rows, n), out_dtype),
        compiler_params=_params(("arbitrary", "arbitrary")),
        name=name,
    )(a, w)


def _stage_weights(i, pairs):
    @pl.when(i == 0)
    def _():
        for w_ref, w_scr in pairs:
            w_scr[...] = w_ref[0].astype(BF16)


def _merge_kernel(oa_c, ob_c, oc_c, oa_d, ob_d, oc_d, ga_ref, gb_ref, gc_ref, wa_ref, wb_ref, wc_ref, o_ref,
                  wa_scr, wb_scr, wc_scr, *, n_a):
    i = pl.program_id(1)
    _stage_weights(i, ((wa_ref, wa_scr), (wb_ref, wb_scr), (wc_ref, wc_scr)))

    def body(oa_ref, ob_ref, oc_ref):
        y = ga_ref[...].astype(F32) * _dot(oa_ref[...], wa_scr[...])
        y += gb_ref[...].astype(F32) * _dot(ob_ref[...], wb_scr[...])
        y += gc_ref[...].astype(F32) * _dot(oc_ref[...], wc_scr[...])
        o_ref[...] = y.astype(o_ref.dtype)

    @pl.when(i < n_a)
    def _():
        body(oa_c, ob_c, oc_c)

    @pl.when(i >= n_a)
    def _():
        body(oa_d, ob_d, oc_d)


def branch_merge(o_ctx, o_dec, gates, wa, wb, wc, layer, tm=512, tn=1024):
    m = M_ALL
    nb = D_MODEL // tn
    n_a = M_CTX // tm
    amap, bmap = _two_source_maps(n_a, False)
    col = lambda j, i: (layer, 0, j)
    widths = (GLA_V_W, DIFF_V_W, NA_W)
    return pl.pallas_call(
        functools.partial(_merge_kernel, n_a=n_a),
        grid=(nb, m // tm),
        in_specs=[pl.BlockSpec((tm, w), lambda j, i: (amap(i), 0)) for w in widths]
                 + [pl.BlockSpec((tm, w), lambda j, i: (bmap(i), 0)) for w in widths]
                 + [pl.BlockSpec((tm, tn), lambda j, i: (i, j)),
                  pl.BlockSpec((tm, tn), lambda j, i: (i, j + nb)),
                  pl.BlockSpec((tm, tn), lambda j, i: (i, j + 2 * nb)),
                  pl.BlockSpec((1, GLA_V_W, tn), col),
                  pl.BlockSpec((1, DIFF_V_W, tn), col),
                  pl.BlockSpec((1, NA_W, tn), col)],
        out_specs=pl.BlockSpec((tm, tn), lambda j, i: (i, j)),
        out_shape=jax.ShapeDtypeStruct((m, D_MODEL), BF16),
        scratch_shapes=[pltpu.VMEM((GLA_V_W, tn), BF16), pltpu.VMEM((DIFF_V_W, tn), BF16),
                        pltpu.VMEM((NA_W, tn), BF16)],
        compiler_params=_params(("arbitrary", "arbitrary")),
        name="branch_merge",
    )(*o_ctx, *o_dec, gates, gates, gates, wa, wb, wc)


def _two_source_maps(n_a, stacked):
    b_off = n_a if stacked else 0
    return (lambda i: jnp.minimum(i, n_a - 1)), (lambda i: jnp.maximum(i - n_a, 0) + b_off)


def _resid_mm_kernel(a_ref, w_ref, xa_ref, xb_ref, gt_ref, o_ref, w_scr, *, n_a):
    i = pl.program_id(1)
    _stage_weights(i, ((w_ref, w_scr),))
    y = gt_ref[0] * _dot(a_ref[...], w_scr[...])

    @pl.when(i < n_a)
    def _():
        o_ref[...] = xa_ref[...] + y

    @pl.when(i >= n_a)
    def _():
        o_ref[...] = xb_ref[...] + y


def resid_matmul(a, w, xs, gt, layer, tm, tn, name, w_single_buffer=False):
    m, k = a.shape
    n = w.shape[2]
    n_a = M_CTX // tm
    amap, bmap = _two_source_maps(n_a, len(xs) == 1)
    xa, xb = (xs[0], xs[0]) if len(xs) == 1 else xs
    wmode = dict(pipeline_mode=pl.Buffered(1)) if w_single_buffer else {}
    return pl.pallas_call(
        functools.partial(_resid_mm_kernel, n_a=n_a),
        grid=(n // tn, m // tm),
        in_specs=[pl.BlockSpec((tm, k), lambda j, i: (i, 0)),
                  pl.BlockSpec((1, k, tn), lambda j, i: (layer, 0, j), **wmode),
                  pl.BlockSpec((tm, tn), lambda j, i: (amap(i), j)),
                  pl.BlockSpec((tm, tn), lambda j, i: (bmap(i), j)),
                  pl.BlockSpec((1, 1, tn), lambda j, i: (_cond_of_row(i * tm), 0, j))],
        out_specs=pl.BlockSpec((tm, tn), lambda j, i: (i, j)),
        out_shape=jax.ShapeDtypeStruct((m, n), F32),
        scratch_shapes=[pltpu.VMEM((k, tn), BF16)],
        compiler_params=_params(("arbitrary", "arbitrary")),
        name=name,
    )(a, w, xa, xb, gt)


def _ffn_up_kernel(h_ref, wg_ref, wu_ref, o_ref, wg_scr, wu_scr):
    _stage_weights(pl.program_id(1), ((wg_ref, wg_scr), (wu_ref, wu_scr)))
    h = h_ref[...]
    g = _dot(h, wg_scr[...])
    u = _dot(h, wu_scr[...])
    o_ref[...] = (g * jax.nn.sigmoid(g) * u).astype(o_ref.dtype)


def ffn_up(h, wg, wu, layer, tm=1024, tn=512):
    m = h.shape[0]
    wspec = pl.BlockSpec((1, D_MODEL, tn), lambda j, i: (layer, 0, j))
    return pl.pallas_call(
        _ffn_up_kernel,
        grid=(D_FF // tn, m // tm),
        in_specs=[pl.BlockSpec((tm, D_MODEL), lambda j, i: (i, 0)), wspec, wspec],
        out_specs=pl.BlockSpec((tm, tn), lambda j, i: (i, j)),
        out_shape=jax.ShapeDtypeStruct((m, D_FF), BF16),
        scratch_shapes=[pltpu.VMEM((D_MODEL, tn), BF16), pltpu.VMEM((D_MODEL, tn), BF16)],
        compiler_params=_params(("arbitrary", "arbitrary")),
        name="ffn_up",
    )(h, wg, wu)


def _gla_kernel(q_ref, k_ref, v_ref, gr_ref, lr_ref, wd_ref, bd_ref, g_ref, *rest, nblk, has_state):
    if has_state:
        s0f_ref, s0b_ref, o_ref, sfin_ref, s_scr, of_scr = rest
    else:
        o_ref, sfin_ref, s_scr, of_scr = rest
    pss = pl.program_id(1)
    blk = pl.program_id(2)
    C = GLA_CHUNK
    nch = GLA_BLOCK // C

    if has_state:
        @pl.when(jnp.logical_and(blk == 0, pss == 0))
        def _init_f():
            s_scr[...] = s0f_ref[0, 0]

        @pl.when(jnp.logical_and(blk == 0, pss == 1))
        def _init_b():
            s_scr[...] = s0b_ref[0, 0]
    else:
        @pl.when(blk == 0)
        def _init():
            s_scr[...] = jnp.zeros_like(s_scr)

    lr = lr_ref[...]
    lr_hi = lr.astype(BF16)
    lr_lo = (lr - lr_hi.astype(F32)).astype(BF16)
    wd = wd_ref[0]
    wd_hi = wd.astype(BF16)
    wd_lo = (wd - wd_hi.astype(F32)).astype(BF16)
    pre = _dot(lr_hi, wd_hi) + _dot(lr_hi, wd_lo) + _dot(lr_lo, wd_hi) + bd_ref[0]
    la_all = jax.nn.log_sigmoid(pre) * (1.0 / GLA_TAU)

    row = lax.broadcasted_iota(jnp.int32, (C, C), 0)
    colm = lax.broadcasted_iota(jnp.int32, (C, C), 1)
    lane = lax.broadcasted_iota(jnp.int32, (1, LANES), 1)
    head_mask = (lane < GLA_DK, lane >= GLA_DK)
    eye = (lax.broadcasted_iota(jnp.int32, (LANES, LANES), 0)
           == lax.broadcasted_iota(jnp.int32, (LANES, LANES), 1))

    def run_direction(backward):
        keep = (colm >= row) if backward else (colm <= row)
        tri = jnp.where(keep, 1.0, 0.0).astype(BF16)
        last, mid = (0, C // 2) if backward else (C - 1, C // 2 - 1)
        order = range(nch - 1, -1, -1) if backward else range(nch)
        for c in order:
            rs = slice(c * C, (c + 1) * C)
            for p in range(GLA_HEADS // 2):
                ls = slice(p * LANES, (p + 1) * LANES)
                la = la_all[rs, ls]
                hi, md, lo = _split3(la)
                b = _dot(tri, hi) + _dot(tri, md) + _dot(tri, lo)
                b_last = b[last:last + 1, :]
                b_mid = b[mid:mid + 1, :]
                q = q_ref[rs, ls] * (GLA_DK ** -0.5)
                k = k_ref[rs, ls]
                q_in = q * jnp.exp(b - b_mid)
                k_in = (k * jnp.exp(b_mid - b)).astype(BF16)
                q_ex = q * jnp.exp(b)
                k_dc = k * jnp.exp(b_last - b)
                dec_col = jnp.exp(jnp.sum(jnp.where(eye, b_last, 0.0), axis=1, keepdims=True))
                s_old = s_scr[p]
                s_old_bf = s_old.astype(BF16)
                s_new = dec_col * s_old
                for h in range(2):
                    hd = 2 * p + h
                    vs = slice(hd * GLA_DV, (hd + 1) * GLA_DV)
                    v = v_ref[rs, vs].astype(BF16)
                    att = _dot_nt(jnp.where(head_mask[h], q_in, 0.0).astype(BF16), k_in)
                    att = jnp.where(keep, att, 0.0).astype(BF16)
                    o = _dot(att, v) + _dot(jnp.where(head_mask[h], q_ex, 0.0).astype(BF16), s_old_bf)
                    s_new = s_new + _dot_tn(jnp.where(head_mask[h], k_dc, 0.0).astype(BF16), v)
                    if backward:
                        tot = of_scr[blk_rows(c), vs] + o
                        y = tot * lax.rsqrt(jnp.mean(tot * tot, axis=-1, keepdims=True) + NORM_EPS) * g_ref[0]
                        gr = gr_ref[rs, vs]
                        o_ref[rs, vs] = (y * (gr * jax.nn.sigmoid(gr))).astype(o_ref.dtype)
                    else:
                        of_scr[blk_rows(c), vs] = o
                s_scr[p] = s_new

    def blk_rows(c):
        pos = jnp.where(pss == 0, blk, nblk - 1 - blk)
        return pl.ds(pl.multiple_of(pos * GLA_BLOCK + c * C, C), C)

    @pl.when(pss == 0)
    def _fwd():
        run_direction(False)

    @pl.when(pss == 1)
    def _bwd():
        run_direction(True)

    @pl.when(blk == nblk - 1)
    def _fin():
        sfin_ref[0, 0] = s_scr[...]


def gla_mixer(pa, wd, bd, g, layer, *, row0, nseq, seqlen, state=None):
    nblk = seqlen // GLA_BLOCK
    b0 = row0 // GLA_BLOCK
    has_state = state is not None

    def rb(s, p, b):
        return b0 + s * nblk + b + p * (nblk - 1 - 2 * b)

    def orb(s, p, b):
        return s * nblk + nblk - 1 - p * b

    in_specs = [pl.BlockSpec((GLA_BLOCK, GLA_QK_W), lambda s, p, b: (rb(s, p, b), 0)),
                pl.BlockSpec((GLA_BLOCK, GLA_QK_W), lambda s, p, b: (rb(s, p, b), 1)),
                pl.BlockSpec((GLA_BLOCK, GLA_V_W), lambda s, p, b: (rb(s, p, b), 1)),
                pl.BlockSpec((GLA_BLOCK, GLA_V_W), lambda s, p, b: (rb(s, p, b), 2)),
                pl.BlockSpec((GLA_BLOCK, LANES), lambda s, p, b: (rb(s, p, b), GLA_A_W // LANES - 1)),
                pl.BlockSpec((1, LANES, GLA_QK_W), lambda s, p, b: (p, 0, 0)),
                pl.BlockSpec((1, 1, GLA_QK_W), lambda s, p, b: (p, 0, 0)),
                pl.BlockSpec((1, 1, GLA_DV), lambda s, p, b: (layer, 0, 0))]
    args = [pa, pa, pa, pa, pa, wd, bd, g]
    if has_state:
        st_spec = pl.BlockSpec((1, 1, 2, LANES, LANES), lambda s, p, b: (s, layer, 0, 0, 0))
        in_specs += [st_spec, st_spec]
        args += [state[0], state[1]]
    kern = functools.partial(_gla_kernel, nblk=nblk, has_state=has_state)
    return pl.pallas_call(
        kern,
        grid=(nseq, 2, nblk),
        in_specs=in_specs,
        out_specs=[pl.BlockSpec((GLA_BLOCK, GLA_V_W), lambda s, p, b: (orb(s, p, b), 0)),
                   pl.BlockSpec((1, 1, 2, LANES, LANES), lambda s, p, b: (p, s, 0, 0, 0))],
        out_shape=[jax.ShapeDtypeStruct((nseq * seqlen, GLA_V_W), BF16),
                   jax.ShapeDtypeStruct((2, nseq, 2, LANES, LANES), F32)],
        scratch_shapes=[pltpu.VMEM((2, LANES, LANES), F32),
                        pltpu.VMEM((seqlen, GLA_V_W), F32)],
        compiler_params=_params(("arbitrary", "arbitrary", "arbitrary")),
        name="gla_mixer",
    )(*args)


def _rope_kernel(x_ref, cos_ref, sin_ref, o_ref):
    cos = cos_ref[...]
    sin = sin_ref[...]
    lane = lax.broadcasted_iota(jnp.int32, (1, LANES), 1)
    first = (lane % (2 * ROPE_NFREQ)) < ROPE_NFREQ
    nq = DIFF_QK_W // LANES
    for j in range(2 * nq):
        x = x_ref[:, j * LANES:(j + 1) * LANES].astype(F32)
        partner = jnp.where(first, pltpu.roll(x, LANES - ROPE_NFREQ, 1), pltpu.roll(x, ROPE_NFREQ, 1))
        y = x * cos + partner * sin
        if j < nq:
            y = y * DIFF_QSCALE
        o_ref[:, j * LANES:(j + 1) * LANES] = y.astype(o_ref.dtype)


def rope_qk(pb_dec, cos_t, sin_t, tm=512):
    nt = DEC_SEQ // tm
    return pl.pallas_call(
        _rope_kernel,
        grid=(M_DEC // tm,),
        in_specs=[pl.BlockSpec((tm, 2 * DIFF_QK_W), lambda i: (i, 0)),
                  pl.BlockSpec((tm, LANES), lambda i: (i % nt, 0)),
                  pl.BlockSpec((tm, LANES), lambda i: (i % nt, 0))],
        out_specs=pl.BlockSpec((tm, 2 * DIFF_QK_W), lambda i: (i, 0)),
        out_shape=jax.ShapeDtypeStruct((M_DEC, 2 * DIFF_QK_W), BF16),
        compiler_params=_params(("arbitrary",)),
        name="rope_qk",
    )(pb_dec, cos_t, sin_t)


def _diff_head(q, chunks, lam, g, lam_init):
    lane = lax.broadcasted_iota(jnp.int32, (1, LANES), 1)
    qm = [jnp.where(lane < DIFF_DK, q, jnp.zeros_like(q)), jnp.where(lane >= DIFF_DK, q, jnp.zeros_like(q))]
    m = [None, None]
    acc = [None, None]
    for k, v in chunks:
        for mp in range(2):
            s = _dot_nt(qm[mp], k)
            cm = jnp.max(s, axis=-1, keepdims=True)
            if m[mp] is None:
                m[mp] = cm
                acc[mp] = _dot(jnp.exp2(s - cm).astype(BF16), v)
            else:
                m_new = jnp.maximum(m[mp], cm)
                acc[mp] = jnp.exp2(m[mp] - m_new) * acc[mp] + _dot(jnp.exp2(s - m_new).astype(BF16), v)
                m[mp] = m_new
    outs = [a[:, :LANES] / a[:, LANES:] for a in acc]
    o = outs[0] - lam * outs[1]
    y = o * lax.rsqrt(jnp.mean(o * o, axis=-1, keepdims=True) + NORM_EPS) * g
    return y * (1.0 - lam_init)


def _diff_kernel(lam_ref, q_ref, kn_ref, vn_ref, kc_ref, vc_ref, g_ref, o_ref, vaug, kc_scr, *, lam_init):
    n_new = kn_ref.shape[0]
    n_c = PAST_LEN

    @pl.when(pl.program_id(2) == 0)
    def _stage():
        vaug[:, LANES:] = jnp.ones((n_c + n_new, LANES), BF16)
        vaug[n_c:, :LANES] = vn_ref[...]
        vaug[:n_c, :LANES] = vc_ref[0, 0].astype(BF16)
        kc_scr[...] = kc_ref[0, 0].astype(BF16)

    chunks = [(kc_scr[...], vaug[:n_c, :])]
    for c in range(n_new // DIFF_KCHUNK):
        rows = slice(c * DIFF_KCHUNK, (c + 1) * DIFF_KCHUNK)
        chunks.append((kn_ref[rows, :], vaug[n_c + c * DIFF_KCHUNK:n_c + (c + 1) * DIFF_KCHUNK, :]))
    y = _diff_head(q_ref[...], chunks, lam_ref[0], g_ref[0], lam_init)
    o_ref[...] = y.astype(o_ref.dtype)


def _diff_ctx_kernel(lam_ref, q_ref, k_ref, v_ref, g_ref, o_ref, *, lam_init):
    ones = jnp.ones((SEQ, LANES), BF16)
    for h in range(DIFF_HEADS):
        ls = slice(h * LANES, (h + 1) * LANES)
        q = (q_ref[:, ls] * DIFF_QSCALE).astype(BF16)
        vaug = jnp.concatenate([v_ref[:, ls].astype(BF16), ones], axis=1)
        y = _diff_head(q, [(k_ref[:, ls].astype(BF16), vaug)], lam_ref[0], g_ref[0], lam_init)
        o_ref[:, ls] = y.astype(o_ref.dtype)


def diff_attention_ctx(pb_ctx, lam, g, layer, lam_init):
    return pl.pallas_call(
        functools.partial(_diff_ctx_kernel, lam_init=lam_init),
        grid=(BATCH,),
        in_specs=[pl.BlockSpec(memory_space=pltpu.SMEM),
                  pl.BlockSpec((SEQ, DIFF_QK_W), lambda s: (s, 0)),
                  pl.BlockSpec((SEQ, DIFF_QK_W), lambda s: (s, 1)),
                  pl.BlockSpec((SEQ, DIFF_V_W), lambda s: (s, 2)),
                  pl.BlockSpec((1, 1, DIFF_DV), lambda s: (layer, 0, 0))],
        out_specs=pl.BlockSpec((SEQ, DIFF_V_W), lambda s: (s, 0)),
        out_shape=jax.ShapeDtypeStruct((M_CTX, DIFF_V_W), BF16),
        compiler_params=_params(("arbitrary",)),
        name="diff_attention_ctx",
    )(lam, pb_ctx, pb_ctx, pb_ctx, g)


def diff_attention_dec(qk_rope, pb_dec, cache_k, cache_v, layer, lam, g, lam_init, tq=512):
    kern = functools.partial(_diff_kernel, lam_init=lam_init)
    H = DIFF_HEADS
    nqb = DEC_SEQ // tq
    ntot = PAST_LEN + DEC_SEQ
    return pl.pallas_call(
        kern,
        grid=(DEC_BATCH, H, nqb),
        in_specs=[pl.BlockSpec(memory_space=pltpu.SMEM),
                  pl.BlockSpec((tq, LANES), lambda s, h, b: (s * nqb + b, h)),
                  pl.BlockSpec((DEC_SEQ, LANES), lambda s, h, b: (s, H + h)),
                  pl.BlockSpec((DEC_SEQ, LANES), lambda s, h, b: (s, 2 * H + h)),
                  pl.BlockSpec((1, 1, PAST_LEN, LANES), lambda s, h, b: (s, layer, 0, h)),
                  pl.BlockSpec((1, 1, PAST_LEN, LANES), lambda s, h, b: (s, layer, 0, h)),
                  pl.BlockSpec((1, 1, DIFF_DV), lambda s, h, b: (layer, 0, 0))],
        out_specs=pl.BlockSpec((tq, LANES), lambda s, h, b: (s * nqb + b, h)),
        out_shape=jax.ShapeDtypeStruct((M_DEC, DIFF_V_W), BF16),
        scratch_shapes=[pltpu.VMEM((ntot, 2 * LANES), BF16),
                        pltpu.VMEM((PAST_LEN, LANES), BF16)],
        compiler_params=_params(("arbitrary", "arbitrary", "arbitrary")),
        name="diff_attention_dec",
    )(lam, qk_rope, qk_rope, pb_dec, cache_k, cache_v, g)


def _softmax_pair(q, keys_vals, biases):
    lane = lax.broadcasted_iota(jnp.int32, (1, LANES), 1)
    outs = []
    for h in range(2):
        msk = (lane < NA_DK) if h == 0 else (lane >= NA_DK)
        qm = jnp.where(msk, q, jnp.zeros_like(q))
        ss = []
        for (k, _), bias in zip(keys_vals, biases):
            s = _dot_nt(qm, k)
            if bias is not None:
                s = s + bias(h)
            ss.append(s)
        m = functools.reduce(jnp.maximum, [jnp.max(s, axis=-1, keepdims=True) for s in ss])
        num = 0.0
        den = 0.0
        for s, (_, v) in zip(ss, keys_vals):
            e = jnp.exp(s - m)
            den = den + jnp.sum(e, axis=-1, keepdims=True)
            num = num + _dot(e.astype(BF16), v)
        outs.append(num / den)
    return jnp.where(lane < NA_DK, outs[0], outs[1])


def _ctx_attn_kernel(q_ref, k_ref, v_ref, o_ref):
    for p in range(NA_HEADS // 2):
        ls = slice(p * LANES, (p + 1) * LANES)
        q = (q_ref[:, ls] * (NA_DK ** -0.5)).astype(BF16)
        k = k_ref[:, ls].astype(BF16)
        v = v_ref[:, ls].astype(BF16)
        o_ref[:, ls] = _softmax_pair(q, [(k, v)], [None]).astype(o_ref.dtype)


def context_attention_ctx(pc_ctx):
    return pl.pallas_call(
        _ctx_attn_kernel,
        grid=(BATCH,),
        in_specs=[pl.BlockSpec((SEQ, NA_W), lambda s: (s, 0)),
                  pl.BlockSpec((SEQ, NA_W), lambda s: (s, 1)),
                  pl.BlockSpec((SEQ, NA_W), lambda s: (s, 2))],
        out_specs=pl.BlockSpec((SEQ, NA_W), lambda s: (s, 0)),
        out_shape=jax.ShapeDtypeStruct((M_CTX, NA_W), BF16),
        compiler_params=_params(("arbitrary",)),
        name="context_attention",
    )(pc_ctx, pc_ctx, pc_ctx)


def _na_window_start(rb):
    return jnp.clip(NA_QROWS * rb - NA_WIN_H // 2, 0, GRID_H - NA_KROWS)


def _na_kernel(q_ref, k_ref, v_ref, kc_ref, vc_ref, bias_ref, o_ref, kc_scr, vc_scr):
    rb = pl.program_id(1)

    @pl.when(rb == 0)
    def _stage():
        kc_scr[...] = kc_ref[0, 0].astype(BF16)
        vc_scr[...] = vc_ref[0, 0].astype(BF16)

    start = pl.multiple_of(_na_window_start(rb) * GRID_W, GRID_W)
    for p in range(NA_HEADS // 2):
        ls = slice(p * LANES, (p + 1) * LANES)
        kl = k_ref[pl.ds(start, NA_NLOC), ls]
        vl = v_ref[pl.ds(start, NA_NLOC), ls]
        q = q_ref[:, ls] * (NA_DK ** -0.5)
        o = _softmax_pair(q, [(kl, vl), (kc_scr[:, ls], vc_scr[:, ls])],
                          [lambda h, p=p: bias_ref[0, 2 * p + h], None])
        o_ref[:, ls] = o.astype(o_ref.dtype)


def _na_situation(rb):
    nrb = GRID_H // NA_QROWS
    return jnp.where(rb < 2, rb, jnp.where(rb >= nrb - 2, rb - (nrb - 5), 2))


def neighbourhood_attention_dec(pc_dec, cache_k, cache_v, layer, bias):
    nrb = GRID_H // NA_QROWS
    cspec = pl.BlockSpec((1, 1, PAST_LEN, NA_W), lambda s, r: (s, layer, 0, 0))
    return pl.pallas_call(
        _na_kernel,
        grid=(DEC_BATCH, nrb),
        in_specs=[pl.BlockSpec((NA_QBLK, NA_W), lambda s, r: (s * nrb + r, 0)),
                  pl.BlockSpec((DEC_SEQ, NA_W), lambda s, r: (s, 1)),
                  pl.BlockSpec((DEC_SEQ, NA_W), lambda s, r: (s, 2)),
                  cspec, cspec,
                  pl.BlockSpec((1, NA_HEADS, NA_QBLK, NA_NLOC), lambda s, r: (_na_situation(r), 0, 0, 0))],
        out_specs=pl.BlockSpec((NA_QBLK, NA_W), lambda s, r: (s * nrb + r, 0)),
        out_shape=jax.ShapeDtypeStruct((M_DEC, NA_W), BF16),
        scratch_shapes=[pltpu.VMEM((PAST_LEN, NA_W), BF16), pltpu.VMEM((PAST_LEN, NA_W), BF16)],
        compiler_params=_params(("arbitrary", "arbitrary")),
        name="neighbourhood_attention",
    )(pc_dec, pc_dec, pc_dec, cache_k, cache_v, bias)


def _na_bias_tables(rpb):
    nrb = GRID_H // NA_QROWS
    rbs = [0, 1, 2, nrb - 2, nrb - 1]
    qc = np.arange(GRID_W)
    kc = np.arange(GRID_W)
    col_start = np.clip(qc - NA_WIN_W // 2, 0, GRID_W - NA_WIN_W)
    col_ok = (kc[None, :] >= col_start[:, None]) & (kc[None, :] < col_start[:, None] + NA_WIN_W)
    dcol = np.clip(kc[None, :] - qc[:, None], 1 - NA_WIN_W, NA_WIN_W - 1) + NA_WIN_W - 1
    drow_all, ok_all = [], []
    for rb in rbs:
        ws = int(np.clip(NA_QROWS * rb - NA_WIN_H // 2, 0, GRID_H - NA_KROWS))
        qr = NA_QROWS * rb + np.arange(NA_QROWS)
        kr = ws + np.arange(NA_KROWS)
        win = np.clip(qr - NA_WIN_H // 2, 0, GRID_H - NA_WIN_H)
        row_ok = (kr[None, :] >= win[:, None]) & (kr[None, :] < win[:, None] + NA_WIN_H)
        drow_all.append(np.clip(kr[None, :] - qr[:, None] + NA_WIN_H - 1, 0, 2 * NA_WIN_H - 2))
        ok_all.append(row_ok[:, None, :, None] & col_ok[None, :, None, :])
    nd, nc = 2 * NA_WIN_H - 1, 2 * NA_WIN_W - 1
    sel_row = (np.stack(drow_all).reshape(-1)[:, None] == np.arange(nd)[None, :]).astype(np.float32)
    sel_col = (np.arange(nc)[:, None] == dcol.reshape(-1)[None, :]).astype(np.float32)
    tab = jnp.einsum("nd,hdc,cx->hnx", sel_row, rpb.astype(F32), sel_col, precision=lax.Precision.HIGHEST)
    tab = tab.reshape(NA_HEADS, 5, NA_QROWS, NA_KROWS, GRID_W, GRID_W)
    tab = jnp.transpose(tab, (1, 0, 2, 4, 3, 5))
    ok = np.stack(ok_all)[:, None]
    return jnp.where(ok, tab, NEG_INF).reshape(5, NA_HEADS, NA_QBLK, NA_NLOC)


def _rope_tables():
    pos = np.arange(DEC_SEQ)
    row = (pos // GRID_W).astype(np.float32)
    col = (pos % GRID_W).astype(np.float32)
    inv = jnp.asarray(ROPE_BASE, F32) ** (-jnp.arange(ROPE_NFREQ, dtype=F32) / ROPE_NFREQ)
    ang_r = jnp.asarray(row)[:, None] * inv
    ang_c = jnp.asarray(col)[:, None] * inv
    cos64 = jnp.concatenate([jnp.cos(ang_r), jnp.cos(ang_r), jnp.cos(ang_c), jnp.cos(ang_c)], axis=1)
    sin64 = jnp.concatenate([-jnp.sin(ang_r), jnp.sin(ang_r), -jnp.sin(ang_c), jnp.sin(ang_c)], axis=1)
    return jnp.tile(cos64, (1, 2)), jnp.tile(sin64, (1, 2))


def kernel(x_prompt, x_sample, c, cache_diff_k, cache_diff_v, cache_na_k, cache_na_v, state_gla_fwd,
           state_gla_bwd, c_ctx, w_ada, b_ada, norm_mix_g, w_in, w_decay, b_decay, gla_norm_g, diff_lambda,
           diff_norm_g, na_rpb, w_br_a, w_br_b, w_br_c, w_out, norm_ffn_g, w_ffn_gate, w_ffn_up, w_ffn_down,
           norm_final_g):
    D = D_MODEL
    xs = (x_prompt.reshape(M_CTX, D), x_sample.reshape(M_DEC, D))
    cond8 = jnp.concatenate([c_ctx[None], c, jnp.zeros((8 - N_COND, D), F32)], axis=0)
    mod = modulation_all(cond8, w_ada, b_ada)
    cos_t, sin_t = _rope_tables()
    g_mix = norm_mix_g.reshape(DEPTH, 1, D)
    g_ffn = norm_ffn_g.reshape(DEPTH, 1, D)
    g_gla = gla_norm_g.reshape(DEPTH, 1, GLA_DV)
    g_diff = diff_norm_g.reshape(DEPTH, 1, DIFF_DV)
    ck_diff = cache_diff_k.reshape(DEC_BATCH, DEPTH, PAST_LEN, DIFF_QK_W)
    cv_diff = cache_diff_v.reshape(DEC_BATCH, DEPTH, PAST_LEN, DIFF_V_W)
    ck_na = cache_na_k.reshape(DEC_BATCH, DEPTH, PAST_LEN, NA_W)
    cv_na = cache_na_v.reshape(DEC_BATCH, DEPTH, PAST_LEN, NA_W)
    gla_state = (state_gla_fwd.reshape(DEC_BATCH, DEPTH, 2, LANES, LANES),
                 state_gla_bwd.reshape(DEC_BATCH, DEPTH, 2, LANES, LANES))

    offs = np.cumsum((0,) + IN_WIDTHS)
    a_end, b_end, c_end = offs[5], offs[8], offs[11]
    new_state = []
    for l in range(DEPTH):
        lam_init = 0.8 - 0.6 * float(np.exp(-0.3 * l))
        m6 = mod[l, :N_COND].reshape(N_COND, 6, 1, D)
        sh_a, sc_a, gt_a, sh_f, sc_f, gt_f = (m6[:, i] for i in range(6))
        wl = w_in[l]
        zpad = lambda n: jnp.zeros((D, n), F32)
        w_all = jnp.concatenate([wl[:, :a_end], zpad(IN_B0 - a_end), wl[:, a_end:c_end],
                                 zpad(IN_G0 - IN_C0 - (c_end - b_end)), wl[:, c_end:]], axis=1).astype(BF16)
        nb_, nc_, ng_ = b_end - a_end, c_end - b_end, 3 * D

        h = norm_mod(xs, g_mix, l, sc_a, sh_a)
        pa = matmul(h, w_all, F32, col0=0, n=GLA_A_W, tm=512, tn=GLA_A_W, name="in_proj_gla")
        pb_ctx = matmul(h, w_all, F32, col0=IN_B0, n=nb_, rows=M_CTX, name="in_proj_diff_ctx")
        pb_dec = matmul(h, w_all, BF16, col0=IN_B0, n=nb_, row0=M_CTX, rows=M_DEC, name="in_proj_diff_dec")
        pc_ctx = matmul(h, w_all, F32, col0=IN_C0, n=nc_, rows=M_CTX, tn=512, name="in_proj_na_ctx")
        pc_dec = matmul(h, w_all, BF16, col0=IN_C0, n=nc_, row0=M_CTX, rows=M_DEC, tn=512, name="in_proj_na_dec")
        gates = matmul(h, w_all, BF16, col0=IN_G0, n=ng_, act="sigmoid", name="in_proj_gates")

        wd = jnp.zeros((2, LANES, GLA_QK_W), F32)
        wd = wd.at[0, :GLA_LR].set(w_decay[l, 0]).at[1, GLA_LR:2 * GLA_LR].set(w_decay[l, 1])
        bd = b_decay[l][:, None, :]
        oa_c, s_fin = gla_mixer(pa, wd, bd, g_gla, l, row0=0, nseq=BATCH, seqlen=SEQ)
        oa_d, _ = gla_mixer(pa, wd, bd, g_gla, l, row0=M_CTX, nseq=DEC_BATCH, seqlen=DEC_SEQ, state=gla_state)

        lp = diff_lambda[l].astype(F32)
        lam = (jnp.exp(jnp.sum(lp[0] * lp[1])) - jnp.exp(jnp.sum(lp[2] * lp[3])) + lam_init).reshape(1)
        ob_c = diff_attention_ctx(pb_ctx, lam, g_diff, l, lam_init)
        qk_rope = rope_qk(pb_dec, cos_t, sin_t)
        ob_d = diff_attention_dec(qk_rope, pb_dec, ck_diff, cv_diff, l, lam, g_diff, lam_init)

        oc_c = context_attention_ctx(pc_ctx)
        oc_d = neighbourhood_attention_dec(pc_dec, ck_na, cv_na, l, _na_bias_tables(na_rpb[l]))

        merged = branch_merge((oa_c, ob_c, oc_c), (oa_d, ob_d, oc_d), gates, w_br_a, w_br_b, w_br_c, l)
        x = resid_matmul(merged, w_out, xs, gt_a, l, tm=512, tn=1024, name="out_proj")
        xs = (x,)

        h2 = norm_mod(xs, g_ffn, l, sc_f, sh_f)
        act = ffn_up(h2, w_ffn_gate, w_ffn_up, l)
        x = resid_matmul(act, w_ffn_down, xs, gt_f, l, tm=256, tn=1024, name="ffn_down", w_single_buffer=True)
        xs = (x,)

        new_state.append((
            pb_ctx[:, DIFF_QK_W:2 * DIFF_QK_W].reshape(BATCH, SEQ, DIFF_HEADS, 2 * DIFF_DK),
            pb_ctx[:, 2 * DIFF_QK_W:].reshape(BATCH, SEQ, DIFF_HEADS, DIFF_DV),
            pc_ctx[:, NA_W:2 * NA_W].reshape(BATCH, SEQ, NA_HEADS, NA_DK),
            pc_ctx[:, 2 * NA_W:].reshape(BATCH, SEQ, NA_HEADS, NA_DK),
            s_fin[0].reshape(BATCH, GLA_HEADS, GLA_DK, GLA_DV),
            s_fin[1].reshape(BATCH, GLA_HEADS, GLA_DK, GLA_DV)))

    gfin = norm_final_g[None]
    y_prompt = final_norm(x, gfin, 0, M_CTX).reshape(BATCH, SEQ, D)
    y_sample = final_norm(x, gfin, M_CTX, M_DEC).reshape(DEC_BATCH, DEC_SEQ, D)
    stacked = tuple(jnp.stack([st[i] for st in new_state], axis=1) for i in range(6))
    return (y_prompt, y_sample) + stacked
```

```python
import functools

import numpy as np
import jax
import jax.numpy as jnp
from jax import lax
from jax.experimental import pallas as pl
from jax.experimental.pallas import tpu as pltpu

F32 = jnp.float32
BF16 = jnp.bfloat16

D_MODEL = 2048
BATCH = 16
SEQ = 256
DEPTH = 2
DEC_BATCH = 2
DEC_SEQ = 4096
PAST_LEN = 512
GRID_W = 64
GRID_H = DEC_SEQ // GRID_W
NORM_EPS = 1e-6
NEG_INF = -1e30
GLA_HEADS = 4
GLA_DK = 64
GLA_DV = 128
GLA_LR = 16
GLA_TAU = 16.0
GLA_CHUNK = 64
GLA_QK_W = GLA_HEADS * GLA_DK
GLA_V_W = GLA_HEADS * GLA_DV
DIFF_HEADS = 8
DIFF_DK = 64
DIFF_DV = 128
DIFF_QK_W = DIFF_HEADS * 2 * DIFF_DK
DIFF_V_W = DIFF_HEADS * DIFF_DV
ROPE_BASE = 10000.0
ROPE_NFREQ = DIFF_DK // 4
NA_HEADS = 8
NA_DK = 64
NA_W = NA_HEADS * NA_DK
NA_WIN_H = 8
NA_WIN_W = 16
D_FF = ((8 * D_MODEL // 3 + 255) // 256) * 256
IN_WIDTHS = (GLA_QK_W, GLA_QK_W, GLA_V_W, GLA_V_W, 2 * GLA_LR,
             DIFF_QK_W, DIFF_QK_W, DIFF_V_W, NA_W, NA_W, NA_W, 3 * D_MODEL)

M_CTX = BATCH * SEQ
M_DEC = DEC_BATCH * DEC_SEQ
M_ALL = M_CTX + M_DEC
N_COND = 1 + DEC_BATCH

LANES = 128
VMEM_LIMIT_BYTES = 56 * 1024 * 1024

GLA_A_W = 13 * LANES
GLA_BLOCK = 256
NA_QROWS = 2
NA_QBLK = NA_QROWS * GRID_W
NA_KROWS = NA_WIN_H + NA_QROWS - 1
NA_NLOC = NA_KROWS * GRID_W
DIFF_KCHUNK = 512
LOG2E = 1.4426950408889634
DIFF_QSCALE = DIFF_DK ** -0.5 * LOG2E


def _params(sem):
    return pltpu.CompilerParams(dimension_semantics=sem, vmem_limit_bytes=VMEM_LIMIT_BYTES)


def _cond_of_row(row):
    return jnp.where(row < M_CTX, 0, 1 + (row - M_CTX) // DEC_SEQ)


def _dot(a, b):
    return jnp.dot(a, b, preferred_element_type=F32)


def _dot_nt(a, b):
    return lax.dot_general(a, b, (((1,), (1,)), ((), ())), preferred_element_type=F32)


def _dot_tn(a, b):
    return lax.dot_general(a, b, (((0,), (0,)), ((), ())), preferred_element_type=F32)


def _split3(x):
    hi = x.astype(BF16)
    r1 = x - hi.astype(F32)
    mid = r1.astype(BF16)
    lo = (r1 - mid.astype(F32)).astype(BF16)
    return hi, mid, lo


def _mod_kernel(c_ref, w_ref, b_ref, o_ref):
    c = c_ref[...]
    a = (c * jax.nn.sigmoid(c)).astype(BF16)
    o_ref[0] = _dot(a, w_ref[0].astype(BF16)) + b_ref[0]


def modulation_all(cond8, w_ada, b_ada):
    n = w_ada.shape[-1]
    tn = 1024
    return pl.pallas_call(
        _mod_kernel,
        grid=(DEPTH, n // tn),
        in_specs=[pl.BlockSpec((8, D_MODEL), lambda l, j: (0, 0)),
                  pl.BlockSpec((1, D_MODEL, tn), lambda l, j: (l, 0, j)),
                  pl.BlockSpec((1, 1, tn), lambda l, j: (l, 0, j))],
        out_specs=pl.BlockSpec((1, 8, tn), lambda l, j: (l, 0, j)),
        out_shape=jax.ShapeDtypeStruct((DEPTH, 8, n), F32),
        compiler_params=_params(("arbitrary", "arbitrary")),
        name="modulation",
    )(cond8, w_ada, b_ada.reshape(DEPTH, 1, n))


def _norm_mod_kernel(xa_ref, xb_ref, g_ref, sc_ref, sh_ref, o_ref, *, n_a):
    def body(x_ref):
        x = x_ref[...]
        y = x * lax.rsqrt(jnp.mean(x * x, axis=-1, keepdims=True) + NORM_EPS) * g_ref[0]
        o_ref[...] = (y * (1.0 + sc_ref[0]) + sh_ref[0]).astype(o_ref.dtype)

    i = pl.program_id(0)

    @pl.when(i < n_a)
    def _():
        body(xa_ref)

    @pl.when(i >= n_a)
    def _():
        body(xb_ref)


def norm_mod(xs, g, layer, sc, sh, tm=512):
    n_a = M_CTX // tm
    amap, bmap = _two_source_maps(n_a, len(xs) == 1)
    xa, xb = (xs[0], xs[0]) if len(xs) == 1 else xs
    cmap = lambda i: (_cond_of_row(i * tm), 0, 0)
    return pl.pallas_call(
        functools.partial(_norm_mod_kernel, n_a=n_a),
        grid=(M_ALL // tm,),
        in_specs=[pl.BlockSpec((tm, D_MODEL), lambda i: (amap(i), 0)),
                  pl.BlockSpec((tm, D_MODEL), lambda i: (bmap(i), 0)),
                  pl.BlockSpec((1, 1, D_MODEL), lambda i: (layer, 0, 0)),
                  pl.BlockSpec((1, 1, D_MODEL), cmap),
                  pl.BlockSpec((1, 1, D_MODEL), cmap)],
        out_specs=pl.BlockSpec((tm, D_MODEL), lambda i: (i, 0)),
        out_shape=jax.ShapeDtypeStruct((M_ALL, D_MODEL), BF16),
        compiler_params=_params(("arbitrary",)),
        name="norm_mod",
    )(xa, xb, g, sc, sh)


def _rmsnorm_kernel(x_ref, g_ref, o_ref):
    x = x_ref[...]
    o_ref[...] = x * lax.rsqrt(jnp.mean(x * x, axis=-1, keepdims=True) + NORM_EPS) * g_ref[...]


def final_norm(x, g, row0, rows, tm=512):
    return pl.pallas_call(
        _rmsnorm_kernel,
        grid=(rows // tm,),
        in_specs=[pl.BlockSpec((tm, D_MODEL), lambda i: (i + row0 // tm, 0)),
                  pl.BlockSpec((1, D_MODEL), lambda i: (0, 0))],
        out_specs=pl.BlockSpec((tm, D_MODEL), lambda i: (i, 0)),
        out_shape=jax.ShapeDtypeStruct((rows, D_MODEL), F32),
        compiler_params=_params(("arbitrary",)),
        name="final_norm",
    )(x, g)


def _mm_kernel(a_ref, w_ref, o_ref, *, act):
    acc = _dot(a_ref[...], w_ref[...])
    if act == "sigmoid":
        acc = jax.nn.sigmoid(acc)
    o_ref[...] = acc.astype(o_ref.dtype)


def matmul(a, w, out_dtype, *, row0=0, rows=None, tm=1024, tn=1024, act=None, name="matmul"):
    k = a.shape[1]
    n = w.shape[1]
    rows = a.shape[0] if rows is None else rows
    tn = min(tn, n)
    assert rows % tm == 0 and row0 % tm == 0 and n % tn == 0
    r0 = row0 // tm
    return pl.pallas_call(
        functools.partial(_mm_kernel, act=act),
        grid=(rows // tm, n // tn),
        in_specs=[pl.BlockSpec((tm, k), lambda i, j: (i + r0, 0)),
                  pl.BlockSpec((k, tn), lambda i, j: (0, j))],
        out_specs=pl.BlockSpec((tm, tn), lambda i, j: (i, j)),
        out_shape=jax.ShapeDtypeStruct((rows, n), out_dtype),
        compiler_params=_params(("arbitrary", "arbitrary")),
        name=name,
    )(a, w)


def _stage_weights(i, pairs):
    @pl.when(i == 0)
    def _():
        for w_ref, w_scr in pairs:
            w_scr[...] = w_ref[0].astype(BF16)


def _merge_kernel(oa_c, ob_c, oc_c, oa_d, ob_d, oc_d, ga_ref, gb_ref, gc_ref, wa_ref, wb_ref, wc_ref, o_ref,
                  wa_scr, wb_scr, wc_scr, *, n_a):
    i = pl.program_id(1)
    _stage_weights(i, ((wa_ref, wa_scr), (wb_ref, wb_scr), (wc_ref, wc_scr)))

    def body(oa_ref, ob_ref, oc_ref):
        y = ga_ref[...].astype(F32) * _dot(oa_ref[...], wa_scr[...])
        y += gb_ref[...].astype(F32) * _dot(ob_ref[...], wb_scr[...])
        y += gc_ref[...].astype(F32) * _dot(oc_ref[...], wc_scr[...])
        o_ref[...] = y.astype(o_ref.dtype)

    @pl.when(i < n_a)
    def _():
        body(oa_c, ob_c, oc_c)

    @pl.when(i >= n_a)
    def _():
        body(oa_d, ob_d, oc_d)


def branch_merge(o_ctx, o_dec, gates, wa, wb, wc, layer, tm=512, tn=1024):
    m = M_ALL
    nb = D_MODEL // tn
    n_a = M_CTX // tm
    amap, bmap = _two_source_maps(n_a, False)
    col = lambda j, i: (layer, 0, j)
    widths = (GLA_V_W, DIFF_V_W, NA_W)
    return pl.pallas_call(
        functools.partial(_merge_kernel, n_a=n_a),
        grid=(nb, m // tm),
        in_specs=[pl.BlockSpec((tm, w), lambda j, i: (amap(i), 0)) for w in widths]
                 + [pl.BlockSpec((tm, w), lambda j, i: (bmap(i), 0)) for w in widths]
                 + [pl.BlockSpec((tm, tn), lambda j, i: (i, j)),
                  pl.BlockSpec((tm, tn), lambda j, i: (i, j + nb)),
                  pl.BlockSpec((tm, tn), lambda j, i: (i, j + 2 * nb)),
                  pl.BlockSpec((1, GLA_V_W, tn), col),
                  pl.BlockSpec((1, DIFF_V_W, tn), col),
                  pl.BlockSpec((1, NA_W, tn), col)],
        out_specs=pl.BlockSpec((tm, tn), lambda j, i: (i, j)),
        out_shape=jax.ShapeDtypeStruct((m, D_MODEL), BF16),
        scratch_shapes=[pltpu.VMEM((GLA_V_W, tn), BF16), pltpu.VMEM((DIFF_V_W, tn), BF16),
                        pltpu.VMEM((NA_W, tn), BF16)],
        compiler_params=_params(("arbitrary", "arbitrary")),
        name="branch_merge",
    )(*o_ctx, *o_dec, gates, gates, gates, wa, wb, wc)


def _two_source_maps(n_a, stacked):
    b_off = n_a if stacked else 0
    return (lambda i: jnp.minimum(i, n_a - 1)), (lambda i: jnp.maximum(i - n_a, 0) + b_off)


def _resid_mm_kernel(a_ref, w_ref, xa_ref, xb_ref, gt_ref, o_ref, w_scr, *, n_a):
    i = pl.program_id(1)
    _stage_weights(i, ((w_ref, w_scr),))
    y = gt_ref[0] * _dot(a_ref[...], w_scr[...])

    @pl.when(i < n_a)
    def _():
        o_ref[...] = xa_ref[...] + y

    @pl.when(i >= n_a)
    def _():
        o_ref[...] = xb_ref[...] + y


def resid_matmul(a, w, xs, gt, layer, tm, tn, name):
    m, k = a.shape
    n = w.shape[2]
    n_a = M_CTX // tm
    amap, bmap = _two_source_maps(n_a, len(xs) == 1)
    xa, xb = (xs[0], xs[0]) if len(xs) == 1 else xs
    return pl.pallas_call(
        functools.partial(_resid_mm_kernel, n_a=n_a),
        grid=(n // tn, m // tm),
        in_specs=[pl.BlockSpec((tm, k), lambda j, i: (i, 0)),
                  pl.BlockSpec((1, k, tn), lambda j, i: (layer, 0, j)),
                  pl.BlockSpec((tm, tn), lambda j, i: (amap(i), j)),
                  pl.BlockSpec((tm, tn), lambda j, i: (bmap(i), j)),
                  pl.BlockSpec((1, 1, tn), lambda j, i: (_cond_of_row(i * tm), 0, j))],
        out_specs=pl.BlockSpec((tm, tn), lambda j, i: (i, j)),
        out_shape=jax.ShapeDtypeStruct((m, n), F32),
        scratch_shapes=[pltpu.VMEM((k, tn), BF16)],
        compiler_params=_params(("arbitrary", "arbitrary")),
        name=name,
    )(a, w, xa, xb, gt)


def _ffn_up_kernel(h_ref, wg_ref, wu_ref, o_ref, wg_scr, wu_scr):
    _stage_weights(pl.program_id(1), ((wg_ref, wg_scr), (wu_ref, wu_scr)))
    h = h_ref[...]
    g = _dot(h, wg_scr[...])
    u = _dot(h, wu_scr[...])
    o_ref[...] = (g * jax.nn.sigmoid(g) * u).astype(o_ref.dtype)


def ffn_up(h, wg, wu, layer, tm=1024, tn=512):
    m = h.shape[0]
    wspec = pl.BlockSpec((1, D_MODEL, tn), lambda j, i: (layer, 0, j))
    return pl.pallas_call(
        _ffn_up_kernel,
        grid=(D_FF // tn, m // tm),
        in_specs=[pl.BlockSpec((tm, D_MODEL), lambda j, i: (i, 0)), wspec, wspec],
        out_specs=pl.BlockSpec((tm, tn), lambda j, i: (i, j)),
        out_shape=jax.ShapeDtypeStruct((m, D_FF), BF16),
        scratch_shapes=[pltpu.VMEM((D_MODEL, tn), BF16), pltpu.VMEM((D_MODEL, tn), BF16)],
        compiler_params=_params(("arbitrary", "arbitrary")),
        name="ffn_up",
    )(h, wg, wu)


def _gla_kernel(q_ref, k_ref, v_ref, gr_ref, lr_ref, wd_ref, bd_ref, g_ref, *rest, nblk, has_state):
    if has_state:
        s0f_ref, s0b_ref, o_ref, sfin_ref, s_scr, of_scr = rest
    else:
        o_ref, sfin_ref, s_scr, of_scr = rest
    pss = pl.program_id(1)
    blk = pl.program_id(2)
    C = GLA_CHUNK
    nch = GLA_BLOCK // C

    if has_state:
        @pl.when(jnp.logical_and(blk == 0, pss == 0))
        def _init_f():
            s_scr[...] = s0f_ref[0, 0]

        @pl.when(jnp.logical_and(blk == 0, pss == 1))
        def _init_b():
            s_scr[...] = s0b_ref[0, 0]
    else:
        @pl.when(blk == 0)
        def _init():
            s_scr[...] = jnp.zeros_like(s_scr)

    lr = lr_ref[...]
    lr_hi = lr.astype(BF16)
    lr_lo = (lr - lr_hi.astype(F32)).astype(BF16)
    wd = wd_ref[0]
    wd_hi = wd.astype(BF16)
    wd_lo = (wd - wd_hi.astype(F32)).astype(BF16)
    pre = _dot(lr_hi, wd_hi) + _dot(lr_hi, wd_lo) + _dot(lr_lo, wd_hi) + bd_ref[0]
    la_all = jax.nn.log_sigmoid(pre) * (1.0 / GLA_TAU)

    row = lax.broadcasted_iota(jnp.int32, (C, C), 0)
    colm = lax.broadcasted_iota(jnp.int32, (C, C), 1)
    lane = lax.broadcasted_iota(jnp.int32, (1, LANES), 1)
    head_mask = (lane < GLA_DK, lane >= GLA_DK)
    eye = (lax.broadcasted_iota(jnp.int32, (LANES, LANES), 0)
           == lax.broadcasted_iota(jnp.int32, (LANES, LANES), 1))

    def run_direction(backward):
        keep = (colm >= row) if backward else (colm <= row)
        tri = jnp.where(keep, 1.0, 0.0).astype(BF16)
        last, mid = (0, C // 2) if backward else (C - 1, C // 2 - 1)
        order = range(nch - 1, -1, -1) if backward else range(nch)
        for c in order:
            rs = slice(c * C, (c + 1) * C)
            for p in range(GLA_HEADS // 2):
                ls = slice(p * LANES, (p + 1) * LANES)
                la = la_all[rs, ls]
                hi, md, lo = _split3(la)
                b = _dot(tri, hi) + _dot(tri, md) + _dot(tri, lo)
                b_last = b[last:last + 1, :]
                b_mid = b[mid:mid + 1, :]
                q = q_ref[rs, ls] * (GLA_DK ** -0.5)
                k = k_ref[rs, ls]
                q_in = q * jnp.exp(b - b_mid)
                k_in = (k * jnp.exp(b_mid - b)).astype(BF16)
                q_ex = q * jnp.exp(b)
                k_dc = k * jnp.exp(b_last - b)
                dec_col = jnp.exp(jnp.sum(jnp.where(eye, b_last, 0.0), axis=1, keepdims=True))
                s_old = s_scr[p]
                s_old_bf = s_old.astype(BF16)
                s_new = dec_col * s_old
                for h in range(2):
                    hd = 2 * p + h
                    vs = slice(hd * GLA_DV, (hd + 1) * GLA_DV)
                    v = v_ref[rs, vs].astype(BF16)
                    att = _dot_nt(jnp.where(head_mask[h], q_in, 0.0).astype(BF16), k_in)
                    att = jnp.where(keep, att, 0.0).astype(BF16)
                    o = _dot(att, v) + _dot(jnp.where(head_mask[h], q_ex, 0.0).astype(BF16), s_old_bf)
                    s_new = s_new + _dot_tn(jnp.where(head_mask[h], k_dc, 0.0).astype(BF16), v)
                    if backward:
                        tot = of_scr[blk_rows(c), vs] + o
                        y = tot * lax.rsqrt(jnp.mean(tot * tot, axis=-1, keepdims=True) + NORM_EPS) * g_ref[0]
                        gr = gr_ref[rs, vs]
                        o_ref[rs, vs] = (y * (gr * jax.nn.sigmoid(gr))).astype(o_ref.dtype)
                    else:
                        of_scr[blk_rows(c), vs] = o
                s_scr[p] = s_new

    def blk_rows(c):
        pos = jnp.where(pss == 0, blk, nblk - 1 - blk)
        return pl.ds(pl.multiple_of(pos * GLA_BLOCK + c * C, C), C)

    @pl.when(pss == 0)
    def _fwd():
        run_direction(False)

    @pl.when(pss == 1)
    def _bwd():
        run_direction(True)

    @pl.when(blk == nblk - 1)
    def _fin():
        sfin_ref[0, 0] = s_scr[...]


def gla_mixer(pa, wd, bd, g, layer, *, row0, nseq, seqlen, state=None):
    nblk = seqlen // GLA_BLOCK
    b0 = row0 // GLA_BLOCK
    has_state = state is not None

    def rb(s, p, b):
        return b0 + s * nblk + b + p * (nblk - 1 - 2 * b)

    def orb(s, p, b):
        return s * nblk + nblk - 1 - p * b

    in_specs = [pl.BlockSpec((GLA_BLOCK, GLA_QK_W), lambda s, p, b: (rb(s, p, b), 0)),
                pl.BlockSpec((GLA_BLOCK, GLA_QK_W), lambda s, p, b: (rb(s, p, b), 1)),
                pl.BlockSpec((GLA_BLOCK, GLA_V_W), lambda s, p, b: (rb(s, p, b), 1)),
                pl.BlockSpec((GLA_BLOCK, GLA_V_W), lambda s, p, b: (rb(s, p, b), 2)),
                pl.BlockSpec((GLA_BLOCK, LANES), lambda s, p, b: (rb(s, p, b), GLA_A_W // LANES - 1)),
                pl.BlockSpec((1, LANES, GLA_QK_W), lambda s, p, b: (p, 0, 0)),
                pl.BlockSpec((1, 1, GLA_QK_W), lambda s, p, b: (p, 0, 0)),
                pl.BlockSpec((1, 1, GLA_DV), lambda s, p, b: (layer, 0, 0))]
    args = [pa, pa, pa, pa, pa, wd, bd, g]
    if has_state:
        st_spec = pl.BlockSpec((1, 1, 2, LANES, LANES), lambda s, p, b: (s, layer, 0, 0, 0))
        in_specs += [st_spec, st_spec]
        args += [state[0], state[1]]
    kern = functools.partial(_gla_kernel, nblk=nblk, has_state=has_state)
    return pl.pallas_call(
        kern,
        grid=(nseq, 2, nblk),
        in_specs=in_specs,
        out_specs=[pl.BlockSpec((GLA_BLOCK, GLA_V_W), lambda s, p, b: (orb(s, p, b), 0)),
                   pl.BlockSpec((1, 1, 2, LANES, LANES), lambda s, p, b: (p, s, 0, 0, 0))],
        out_shape=[jax.ShapeDtypeStruct((nseq * seqlen, GLA_V_W), BF16),
                   jax.ShapeDtypeStruct((2, nseq, 2, LANES, LANES), F32)],
        scratch_shapes=[pltpu.VMEM((2, LANES, LANES), F32),
                        pltpu.VMEM((seqlen, GLA_V_W), F32)],
        compiler_params=_params(("arbitrary", "arbitrary", "arbitrary")),
        name="gla_mixer",
    )(*args)


def _rope_kernel(x_ref, cos_ref, sin_ref, o_ref):
    cos = cos_ref[...]
    sin = sin_ref[...]
    lane = lax.broadcasted_iota(jnp.int32, (1, LANES), 1)
    first = (lane % (2 * ROPE_NFREQ)) < ROPE_NFREQ
    nq = DIFF_QK_W // LANES
    for j in range(2 * nq):
        x = x_ref[:, j * LANES:(j + 1) * LANES].astype(F32)
        partner = jnp.where(first, pltpu.roll(x, LANES - ROPE_NFREQ, 1), pltpu.roll(x, ROPE_NFREQ, 1))
        y = x * cos + partner * sin
        if j < nq:
            y = y * DIFF_QSCALE
        o_ref[:, j * LANES:(j + 1) * LANES] = y.astype(o_ref.dtype)


def rope_qk(pb_dec, cos_t, sin_t, tm=512):
    nt = DEC_SEQ // tm
    return pl.pallas_call(
        _rope_kernel,
        grid=(M_DEC // tm,),
        in_specs=[pl.BlockSpec((tm, 2 * DIFF_QK_W), lambda i: (i, 0)),
                  pl.BlockSpec((tm, LANES), lambda i: (i % nt, 0)),
                  pl.BlockSpec((tm, LANES), lambda i: (i % nt, 0))],
        out_specs=pl.BlockSpec((tm, 2 * DIFF_QK_W), lambda i: (i, 0)),
        out_shape=jax.ShapeDtypeStruct((M_DEC, 2 * DIFF_QK_W), BF16),
        compiler_params=_params(("arbitrary",)),
        name="rope_qk",
    )(pb_dec, cos_t, sin_t)


def _diff_head(q, chunks, lam, g, lam_init):
    lane = lax.broadcasted_iota(jnp.int32, (1, LANES), 1)
    qm = [jnp.where(lane < DIFF_DK, q, jnp.zeros_like(q)), jnp.where(lane >= DIFF_DK, q, jnp.zeros_like(q))]
    m = [None, None]
    acc = [None, None]
    for k, v in chunks:
        for mp in range(2):
            s = _dot_nt(qm[mp], k)
            cm = jnp.max(s, axis=-1, keepdims=True)
            if m[mp] is None:
                m[mp] = cm
                acc[mp] = _dot(jnp.exp2(s - cm).astype(BF16), v)
            else:
                m_new = jnp.maximum(m[mp], cm)
                acc[mp] = jnp.exp2(m[mp] - m_new) * acc[mp] + _dot(jnp.exp2(s - m_new).astype(BF16), v)
                m[mp] = m_new
    outs = [a[:, :LANES] / a[:, LANES:] for a in acc]
    o = outs[0] - lam * outs[1]
    y = o * lax.rsqrt(jnp.mean(o * o, axis=-1, keepdims=True) + NORM_EPS) * g
    return y * (1.0 - lam_init)


def _diff_kernel(lam_ref, q_ref, kn_ref, vn_ref, kc_ref, vc_ref, g_ref, o_ref, vaug, kc_scr, *, lam_init):
    n_new = kn_ref.shape[0]
    n_c = PAST_LEN

    @pl.when(pl.program_id(2) == 0)
    def _stage():
        vaug[:, LANES:] = jnp.ones((n_c + n_new, LANES), BF16)
        vaug[n_c:, :LANES] = vn_ref[...]
        vaug[:n_c, :LANES] = vc_ref[0, 0].astype(BF16)
        kc_scr[...] = kc_ref[0, 0].astype(BF16)

    chunks = [(kc_scr[...], vaug[:n_c, :])]
    for c in range(n_new // DIFF_KCHUNK):
        rows = slice(c * DIFF_KCHUNK, (c + 1) * DIFF_KCHUNK)
        chunks.append((kn_ref[rows, :], vaug[n_c + c * DIFF_KCHUNK:n_c + (c + 1) * DIFF_KCHUNK, :]))
    y = _diff_head(q_ref[...], chunks, lam_ref[0], g_ref[0], lam_init)
    o_ref[...] = y.astype(o_ref.dtype)


def _diff_ctx_kernel(lam_ref, q_ref, k_ref, v_ref, g_ref, o_ref, *, lam_init):
    ones = jnp.ones((SEQ, LANES), BF16)
    for h in range(DIFF_HEADS):
        ls = slice(h * LANES, (h + 1) * LANES)
        q = (q_ref[:, ls] * DIFF_QSCALE).astype(BF16)
        vaug = jnp.concatenate([v_ref[:, ls].astype(BF16), ones], axis=1)
        y = _diff_head(q, [(k_ref[:, ls].astype(BF16), vaug)], lam_ref[0], g_ref[0], lam_init)
        o_ref[:, ls] = y.astype(o_ref.dtype)


def diff_attention_ctx(pb_ctx, lam, g, layer, lam_init):
    return pl.pallas_call(
        functools.partial(_diff_ctx_kernel, lam_init=lam_init),
        grid=(BATCH,),
        in_specs=[pl.BlockSpec(memory_space=pltpu.SMEM),
                  pl.BlockSpec((SEQ, DIFF_QK_W), lambda s: (s, 0)),
                  pl.BlockSpec((SEQ, DIFF_QK_W), lambda s: (s, 1)),
                  pl.BlockSpec((SEQ, DIFF_V_W), lambda s: (s, 2)),
                  pl.BlockSpec((1, 1, DIFF_DV), lambda s: (layer, 0, 0))],
        out_specs=pl.BlockSpec((SEQ, DIFF_V_W), lambda s: (s, 0)),
        out_shape=jax.ShapeDtypeStruct((M_CTX, DIFF_V_W), BF16),
        compiler_params=_params(("arbitrary",)),
        name="diff_attention_ctx",
    )(lam, pb_ctx, pb_ctx, pb_ctx, g)


def diff_attention_dec(qk_rope, pb_dec, cache_k, cache_v, layer, lam, g, lam_init, tq=1024):
    kern = functools.partial(_diff_kernel, lam_init=lam_init)
    H = DIFF_HEADS
    nqb = DEC_SEQ // tq
    ntot = PAST_LEN + DEC_SEQ
    return pl.pallas_call(
        kern,
        grid=(DEC_BATCH, H, nqb),
        in_specs=[pl.BlockSpec(memory_space=pltpu.SMEM),
                  pl.BlockSpec((tq, LANES), lambda s, h, b: (s * nqb + b, h)),
                  pl.BlockSpec((DEC_SEQ, LANES), lambda s, h, b: (s, H + h)),
                  pl.BlockSpec((DEC_SEQ, LANES), lambda s, h, b: (s, 2 * H + h)),
                  pl.BlockSpec((1, 1, PAST_LEN, LANES), lambda s, h, b: (s, layer, 0, h)),
                  pl.BlockSpec((1, 1, PAST_LEN, LANES), lambda s, h, b: (s, layer, 0, h)),
                  pl.BlockSpec((1, 1, DIFF_DV), lambda s, h, b: (layer, 0, 0))],
        out_specs=pl.BlockSpec((tq, LANES), lambda s, h, b: (s * nqb + b, h)),
        out_shape=jax.ShapeDtypeStruct((M_DEC, DIFF_V_W), BF16),
        scratch_shapes=[pltpu.VMEM((ntot, 2 * LANES), BF16),
                        pltpu.VMEM((PAST_LEN, LANES), BF16)],
        compiler_params=_params(("arbitrary", "arbitrary", "arbitrary")),
        name="diff_attention_dec",
    )(lam, qk_rope, qk_rope, pb_dec, cache_k, cache_v, g)


def _softmax_pair(q, keys_vals, biases):
    lane = lax.broadcasted_iota(jnp.int32, (1, LANES), 1)
    outs = []
    for h in range(2):
        msk = (lane < NA_DK) if h == 0 else (lane >= NA_DK)
        qm = jnp.where(msk, q, jnp.zeros_like(q))
        ss = []
        for (k, _), bias in zip(keys_vals, biases):
            s = _dot_nt(qm, k)
            if bias is not None:
                s = s + bias(h)
            ss.append(s)
        m = functools.reduce(jnp.maximum, [jnp.max(s, axis=-1, keepdims=True) for s in ss])
        num = 0.0
        den = 0.0
        for s, (_, v) in zip(ss, keys_vals):
            e = jnp.exp(s - m)
            den = den + jnp.sum(e, axis=-1, keepdims=True)
            num = num + _dot(e.astype(BF16), v)
        outs.append(num / den)
    return jnp.where(lane < NA_DK, outs[0], outs[1])


def _ctx_attn_kernel(q_ref, k_ref, v_ref, o_ref):
    for p in range(NA_HEADS // 2):
        ls = slice(p * LANES, (p + 1) * LANES)
        q = (q_ref[:, ls] * (NA_DK ** -0.5)).astype(BF16)
        k = k_ref[:, ls].astype(BF16)
        v = v_ref[:, ls].astype(BF16)
        o_ref[:, ls] = _softmax_pair(q, [(k, v)], [None]).astype(o_ref.dtype)


def context_attention_ctx(pc_ctx):
    return pl.pallas_call(
        _ctx_attn_kernel,
        grid=(BATCH,),
        in_specs=[pl.BlockSpec((SEQ, NA_W), lambda s: (s, 0)),
                  pl.BlockSpec((SEQ, NA_W), lambda s: (s, 1)),
                  pl.BlockSpec((SEQ, NA_W), lambda s: (s, 2))],
        out_specs=pl.BlockSpec((SEQ, NA_W), lambda s: (s, 0)),
        out_shape=jax.ShapeDtypeStruct((M_CTX, NA_W), BF16),
        compiler_params=_params(("arbitrary",)),
        name="context_attention",
    )(pc_ctx, pc_ctx, pc_ctx)


def _na_window_start(rb):
    return jnp.clip(NA_QROWS * rb - NA_WIN_H // 2, 0, GRID_H - NA_KROWS)


def _na_kernel(q_ref, k_ref, v_ref, kc_ref, vc_ref, bias_ref, o_ref, kc_scr, vc_scr):
    rb = pl.program_id(1)

    @pl.when(rb == 0)
    def _stage():
        kc_scr[...] = kc_ref[0, 0].astype(BF16)
        vc_scr[...] = vc_ref[0, 0].astype(BF16)

    start = pl.multiple_of(_na_window_start(rb) * GRID_W, GRID_W)
    for p in range(NA_HEADS // 2):
        ls = slice(p * LANES, (p + 1) * LANES)
        kl = k_ref[pl.ds(start, NA_NLOC), ls]
        vl = v_ref[pl.ds(start, NA_NLOC), ls]
        q = q_ref[:, ls] * (NA_DK ** -0.5)
        o = _softmax_pair(q, [(kl, vl), (kc_scr[:, ls], vc_scr[:, ls])],
                          [lambda h, p=p: bias_ref[0, 2 * p + h], None])
        o_ref[:, ls] = o.astype(o_ref.dtype)


def _na_situation(rb):
    nrb = GRID_H // NA_QROWS
    return jnp.where(rb < 2, rb, jnp.where(rb >= nrb - 2, rb - (nrb - 5), 2))


def neighbourhood_attention_dec(pc_dec, cache_k, cache_v, layer, bias):
    nrb = GRID_H // NA_QROWS
    cspec = pl.BlockSpec((1, 1, PAST_LEN, NA_W), lambda s, r: (s, layer, 0, 0))
    return pl.pallas_call(
        _na_kernel,
        grid=(DEC_BATCH, nrb),
        in_specs=[pl.BlockSpec((NA_QBLK, NA_W), lambda s, r: (s * nrb + r, 0)),
                  pl.BlockSpec((DEC_SEQ, NA_W), lambda s, r: (s, 1)),
                  pl.BlockSpec((DEC_SEQ, NA_W), lambda s, r: (s, 2)),
                  cspec, cspec,
                  pl.BlockSpec((1, NA_HEADS, NA_QBLK, NA_NLOC), lambda s, r: (_na_situation(r), 0, 0, 0))],
        out_specs=pl.BlockSpec((NA_QBLK, NA_W), lambda s, r: (s * nrb + r, 0)),
        out_shape=jax.ShapeDtypeStruct((M_DEC, NA_W), BF16),
        scratch_shapes=[pltpu.VMEM((PAST_LEN, NA_W), BF16), pltpu.VMEM((PAST_LEN, NA_W), BF16)],
        compiler_params=_params(("arbitrary", "arbitrary")),
        name="neighbourhood_attention",
    )(pc_dec, pc_dec, pc_dec, cache_k, cache_v, bias)


def _na_bias_tables(rpb):
    nrb = GRID_H // NA_QROWS
    rbs = [0, 1, 2, nrb - 2, nrb - 1]
    qc = np.arange(GRID_W)
    kc = np.arange(GRID_W)
    col_start = np.clip(qc - NA_WIN_W // 2, 0, GRID_W - NA_WIN_W)
    col_ok = (kc[None, :] >= col_start[:, None]) & (kc[None, :] < col_start[:, None] + NA_WIN_W)
    dcol = np.clip(kc[None, :] - qc[:, None], 1 - NA_WIN_W, NA_WIN_W - 1) + NA_WIN_W - 1
    drow_all, ok_all = [], []
    for rb in rbs:
        ws = int(np.clip(NA_QROWS * rb - NA_WIN_H // 2, 0, GRID_H - NA_KROWS))
        qr = NA_QROWS * rb + np.arange(NA_QROWS)
        kr = ws + np.arange(NA_KROWS)
        win = np.clip(qr - NA_WIN_H // 2, 0, GRID_H - NA_WIN_H)
        row_ok = (kr[None, :] >= win[:, None]) & (kr[None, :] < win[:, None] + NA_WIN_H)
        drow_all.append(np.clip(kr[None, :] - qr[:, None] + NA_WIN_H - 1, 0, 2 * NA_WIN_H - 2))
        ok_all.append(row_ok[:, None, :, None] & col_ok[None, :, None, :])
    nd, nc = 2 * NA_WIN_H - 1, 2 * NA_WIN_W - 1
    sel_row = (np.stack(drow_all).reshape(-1)[:, None] == np.arange(nd)[None, :]).astype(np.float32)
    sel_col = (np.arange(nc)[:, None] == dcol.reshape(-1)[None, :]).astype(np.float32)
    tab = jnp.einsum("nd,hdc,cx->hnx", sel_row, rpb.astype(F32), sel_col, precision=lax.Precision.HIGHEST)
    tab = tab.reshape(NA_HEADS, 5, NA_QROWS, NA_KROWS, GRID_W, GRID_W)
    tab = jnp.transpose(tab, (1, 0, 2, 4, 3, 5))
    ok = np.stack(ok_all)[:, None]
    return jnp.where(ok, tab, NEG_INF).reshape(5, NA_HEADS, NA_QBLK, NA_NLOC)


def _rope_tables():
    pos = np.arange(DEC_SEQ)
    row = (pos // GRID_W).astype(np.float32)
    col = (pos % GRID_W).astype(np.float32)
    inv = jnp.asarray(ROPE_BASE, F32) ** (-jnp.arange(ROPE_NFREQ, dtype=F32) / ROPE_NFREQ)
    ang_r = jnp.asarray(row)[:, None] * inv
    ang_c = jnp.asarray(col)[:, None] * inv
    cos64 = jnp.concatenate([jnp.cos(ang_r), jnp.cos(ang_r), jnp.cos(ang_c), jnp.cos(ang_c)], axis=1)
    sin64 = jnp.concatenate([-jnp.sin(ang_r), jnp.sin(ang_r), -jnp.sin(ang_c), jnp.sin(ang_c)], axis=1)
    return jnp.tile(cos64, (1, 2)), jnp.tile(sin64, (1, 2))


def kernel(x_prompt, x_sample, c, cache_diff_k, cache_diff_v, cache_na_k, cache_na_v, state_gla_fwd,
           state_gla_bwd, c_ctx, w_ada, b_ada, norm_mix_g, w_in, w_decay, b_decay, gla_norm_g, diff_lambda,
           diff_norm_g, na_rpb, w_br_a, w_br_b, w_br_c, w_out, norm_ffn_g, w_ffn_gate, w_ffn_up, w_ffn_down,
           norm_final_g):
    D = D_MODEL
    xs = (x_prompt.reshape(M_CTX, D), x_sample.reshape(M_DEC, D))
    cond8 = jnp.concatenate([c_ctx[None], c, jnp.zeros((8 - N_COND, D), F32)], axis=0)
    mod = modulation_all(cond8, w_ada, b_ada)
    cos_t, sin_t = _rope_tables()
    g_mix = norm_mix_g.reshape(DEPTH, 1, D)
    g_ffn = norm_ffn_g.reshape(DEPTH, 1, D)
    g_gla = gla_norm_g.reshape(DEPTH, 1, GLA_DV)
    g_diff = diff_norm_g.reshape(DEPTH, 1, DIFF_DV)
    ck_diff = cache_diff_k.reshape(DEC_BATCH, DEPTH, PAST_LEN, DIFF_QK_W)
    cv_diff = cache_diff_v.reshape(DEC_BATCH, DEPTH, PAST_LEN, DIFF_V_W)
    ck_na = cache_na_k.reshape(DEC_BATCH, DEPTH, PAST_LEN, NA_W)
    cv_na = cache_na_v.reshape(DEC_BATCH, DEPTH, PAST_LEN, NA_W)
    gla_state = (state_gla_fwd.reshape(DEC_BATCH, DEPTH, 2, LANES, LANES),
                 state_gla_bwd.reshape(DEC_BATCH, DEPTH, 2, LANES, LANES))

    offs = np.cumsum((0,) + IN_WIDTHS)
    a_end, b_end, c_end = offs[5], offs[8], offs[11]
    new_state = []
    for l in range(DEPTH):
        lam_init = 0.8 - 0.6 * float(np.exp(-0.3 * l))
        m6 = mod[l, :N_COND].reshape(N_COND, 6, 1, D)
        sh_a, sc_a, gt_a, sh_f, sc_f, gt_f = (m6[:, i] for i in range(6))
        wl = w_in[l]
        w_a = jnp.concatenate([wl[:, :a_end], jnp.zeros((D, GLA_A_W - a_end), F32)], axis=1).astype(BF16)
        w_b = wl[:, a_end:b_end].astype(BF16)
        w_c = wl[:, b_end:c_end].astype(BF16)
        w_g = wl[:, c_end:].astype(BF16)

        h = norm_mod(xs, g_mix, l, sc_a, sh_a)
        pa = matmul(h, w_a, F32, tm=512, tn=GLA_A_W, name="in_proj_gla")
        pb_ctx = matmul(h, w_b, F32, rows=M_CTX, name="in_proj_diff_ctx")
        pb_dec = matmul(h, w_b, BF16, row0=M_CTX, rows=M_DEC, name="in_proj_diff_dec")
        pc_ctx = matmul(h, w_c, F32, rows=M_CTX, tn=768, name="in_proj_na_ctx")
        pc_dec = matmul(h, w_c, BF16, row0=M_CTX, rows=M_DEC, tn=768, name="in_proj_na_dec")
        gates = matmul(h, w_g, BF16, act="sigmoid", name="in_proj_gates")

        wd = jnp.zeros((2, LANES, GLA_QK_W), F32)
        wd = wd.at[0, :GLA_LR].set(w_decay[l, 0]).at[1, GLA_LR:2 * GLA_LR].set(w_decay[l, 1])
        bd = b_decay[l][:, None, :]
        oa_c, s_fin = gla_mixer(pa, wd, bd, g_gla, l, row0=0, nseq=BATCH, seqlen=SEQ)
        oa_d, _ = gla_mixer(pa, wd, bd, g_gla, l, row0=M_CTX, nseq=DEC_BATCH, seqlen=DEC_SEQ, state=gla_state)

        lp = diff_lambda[l].astype(F32)
        lam = (jnp.exp(jnp.sum(lp[0] * lp[1])) - jnp.exp(jnp.sum(lp[2] * lp[3])) + lam_init).reshape(1)
        ob_c = diff_attention_ctx(pb_ctx, lam, g_diff, l, lam_init)
        qk_rope = rope_qk(pb_dec, cos_t, sin_t)
        ob_d = diff_attention_dec(qk_rope, pb_dec, ck_diff, cv_diff, l, lam, g_diff, lam_init)

        oc_c = context_attention_ctx(pc_ctx)
        oc_d = neighbourhood_attention_dec(pc_dec, ck_na, cv_na, l, _na_bias_tables(na_rpb[l]))

        merged = branch_merge((oa_c, ob_c, oc_c), (oa_d, ob_d, oc_d), gates, w_br_a, w_br_b, w_br_c, l)
        x = resid_matmul(merged, w_out, xs, gt_a, l, tm=512, tn=1024, name="out_proj")
        xs = (x,)

        h2 = norm_mod(xs, g_ffn, l, sc_f, sh_f)
        act = ffn_up(h2, w_ffn_gate, w_ffn_up, l)
        x = resid_matmul(act, w_ffn_down, xs, gt_f, l, tm=512, tn=512, name="ffn_down")
        xs = (x,)

        new_state.append((
            pb_ctx[:, DIFF_QK_W:2 * DIFF_QK_W].reshape(BATCH, SEQ, DIFF_HEADS, 2 * DIFF_DK),
            pb_ctx[:, 2 * DIFF_QK_W:].reshape(BATCH, SEQ, DIFF_HEADS, DIFF_DV),
            pc_ctx[:, NA_W:2 * NA_W].reshape(BATCH, SEQ, NA_HEADS, NA_DK),
            pc_ctx[:, 2 * NA_W:].reshape(BATCH, SEQ, NA_HEADS, NA_DK),
            s_fin[0].reshape(BATCH, GLA_HEADS, GLA_DK, GLA_DV),
            s_fin[1].reshape(BATCH, GLA_HEADS, GLA_DK, GLA_DV)))

    gfin = norm_final_g[None]
    y_prompt = final_norm(x, gfin, 0, M_CTX).reshape(BATCH, SEQ, D)
    y_sample = final_norm(x, gfin, M_CTX, M_DEC).reshape(DEC_BATCH, DEC_SEQ, D)
    stacked = tuple(jnp.stack([st[i] for st in new_state], axis=1) for i in range(6))
    return (y_prompt, y_sample) + stacked
```

```python
import functools

import numpy as np
import jax
import jax.numpy as jnp
from jax import lax
from jax.experimental import pallas as pl
from jax.experimental.pallas import tpu as pltpu

F32 = jnp.float32
BF16 = jnp.bfloat16

D_MODEL = 2048
BATCH = 16
SEQ = 256
DEPTH = 2
DEC_BATCH = 2
DEC_SEQ = 4096
PAST_LEN = 512
GRID_W = 64
GRID_H = DEC_SEQ // GRID_W
NORM_EPS = 1e-6
NEG_INF = -1e30
GLA_HEADS = 4
GLA_DK = 64
GLA_DV = 128
GLA_LR = 16
GLA_TAU = 16.0
GLA_CHUNK = 64
GLA_QK_W = GLA_HEADS * GLA_DK
GLA_V_W = GLA_HEADS * GLA_DV
DIFF_HEADS = 8
DIFF_DK = 64
DIFF_DV = 128
DIFF_QK_W = DIFF_HEADS * 2 * DIFF_DK
DIFF_V_W = DIFF_HEADS * DIFF_DV
ROPE_BASE = 10000.0
ROPE_NFREQ = DIFF_DK // 4
NA_HEADS = 8
NA_DK = 64
NA_W = NA_HEADS * NA_DK
NA_WIN_H = 8
NA_WIN_W = 16
D_FF = ((8 * D_MODEL // 3 + 255) // 256) * 256
IN_WIDTHS = (GLA_QK_W, GLA_QK_W, GLA_V_W, GLA_V_W, 2 * GLA_LR,
             DIFF_QK_W, DIFF_QK_W, DIFF_V_W, NA_W, NA_W, NA_W, 3 * D_MODEL)

M_CTX = BATCH * SEQ
M_DEC = DEC_BATCH * DEC_SEQ
M_ALL = M_CTX + M_DEC
N_COND = 1 + DEC_BATCH

LANES = 128
VMEM_LIMIT_BYTES = 56 * 1024 * 1024

GLA_A_W = 13 * LANES
GLA_BLOCK = 256
NA_QROWS = 2
NA_QBLK = NA_QROWS * GRID_W
NA_KROWS = NA_WIN_H + NA_QROWS - 1
NA_NLOC = NA_KROWS * GRID_W
DIFF_KCHUNK = 512
LOG2E = 1.4426950408889634
DIFF_QSCALE = DIFF_DK ** -0.5 * LOG2E


def _params(sem):
    return pltpu.CompilerParams(dimension_semantics=sem, vmem_limit_bytes=VMEM_LIMIT_BYTES)


def _cond_of_row(row):
    return jnp.where(row < M_CTX, 0, 1 + (row - M_CTX) // DEC_SEQ)


def _dot(a, b):
    return jnp.dot(a, b, preferred_element_type=F32)


def _dot_nt(a, b):
    return lax.dot_general(a, b, (((1,), (1,)), ((), ())), preferred_element_type=F32)


def _dot_tn(a, b):
    return lax.dot_general(a, b, (((0,), (0,)), ((), ())), preferred_element_type=F32)


def _split3(x):
    hi = x.astype(BF16)
    r1 = x - hi.astype(F32)
    mid = r1.astype(BF16)
    lo = (r1 - mid.astype(F32)).astype(BF16)
    return hi, mid, lo


def _mod_kernel(c_ref, w_ref, b_ref, o_ref):
    c = c_ref[...]
    a = (c * jax.nn.sigmoid(c)).astype(BF16)
    o_ref[0] = _dot(a, w_ref[0].astype(BF16)) + b_ref[0]


def modulation_all(cond8, w_ada, b_ada):
    n = w_ada.shape[-1]
    tn = 1024
    return pl.pallas_call(
        _mod_kernel,
        grid=(DEPTH, n // tn),
        in_specs=[pl.BlockSpec((8, D_MODEL), lambda l, j: (0, 0)),
                  pl.BlockSpec((1, D_MODEL, tn), lambda l, j: (l, 0, j)),
                  pl.BlockSpec((1, 1, tn), lambda l, j: (l, 0, j))],
        out_specs=pl.BlockSpec((1, 8, tn), lambda l, j: (l, 0, j)),
        out_shape=jax.ShapeDtypeStruct((DEPTH, 8, n), F32),
        compiler_params=_params(("arbitrary", "arbitrary")),
        name="modulation",
    )(cond8, w_ada, b_ada.reshape(DEPTH, 1, n))


def _norm_mod_kernel(xa_ref, xb_ref, g_ref, sc_ref, sh_ref, o_ref, *, n_a):
    def body(x_ref):
        x = x_ref[...]
        y = x * lax.rsqrt(jnp.mean(x * x, axis=-1, keepdims=True) + NORM_EPS) * g_ref[0]
        o_ref[...] = (y * (1.0 + sc_ref[0]) + sh_ref[0]).astype(o_ref.dtype)

    i = pl.program_id(0)

    @pl.when(i < n_a)
    def _():
        body(xa_ref)

    @pl.when(i >= n_a)
    def _():
        body(xb_ref)


def norm_mod(xs, g, layer, sc, sh, tm=512):
    n_a = M_CTX // tm
    amap, bmap = _two_source_maps(n_a, len(xs) == 1)
    xa, xb = (xs[0], xs[0]) if len(xs) == 1 else xs
    cmap = lambda i: (_cond_of_row(i * tm), 0, 0)
    return pl.pallas_call(
        functools.partial(_norm_mod_kernel, n_a=n_a),
        grid=(M_ALL // tm,),
        in_specs=[pl.BlockSpec((tm, D_MODEL), lambda i: (amap(i), 0)),
                  pl.BlockSpec((tm, D_MODEL), lambda i: (bmap(i), 0)),
                  pl.BlockSpec((1, 1, D_MODEL), lambda i: (layer, 0, 0)),
                  pl.BlockSpec((1, 1, D_MODEL), cmap),
                  pl.BlockSpec((1, 1, D_MODEL), cmap)],
        out_specs=pl.BlockSpec((tm, D_MODEL), lambda i: (i, 0)),
        out_shape=jax.ShapeDtypeStruct((M_ALL, D_MODEL), BF16),
        compiler_params=_params(("arbitrary",)),
        name="norm_mod",
    )(xa, xb, g, sc, sh)


def _rmsnorm_kernel(x_ref, g_ref, o_ref):
    x = x_ref[...]
    o_ref[...] = x * lax.rsqrt(jnp.mean(x * x, axis=-1, keepdims=True) + NORM_EPS) * g_ref[...]


def final_norm(x, g, row0, rows, tm=512):
    return pl.pallas_call(
        _rmsnorm_kernel,
        grid=(rows // tm,),
        in_specs=[pl.BlockSpec((tm, D_MODEL), lambda i: (i + row0 // tm, 0)),
                  pl.BlockSpec((1, D_MODEL), lambda i: (0, 0))],
        out_specs=pl.BlockSpec((tm, D_MODEL), lambda i: (i, 0)),
        out_shape=jax.ShapeDtypeStruct((rows, D_MODEL), F32),
        compiler_params=_params(("arbitrary",)),
        name="final_norm",
    )(x, g)


def _mm_kernel(a_ref, w_ref, o_ref, *, act):
    acc = _dot(a_ref[...], w_ref[...])
    if act == "sigmoid":
        acc = jax.nn.sigmoid(acc)
    o_ref[...] = acc.astype(o_ref.dtype)


def matmul(a, w, out_dtype, *, row0=0, rows=None, tm=1024, tn=1024, act=None, name="matmul"):
    k = a.shape[1]
    n = w.shape[1]
    rows = a.shape[0] if rows is None else rows
    tn = min(tn, n)
    assert rows % tm == 0 and row0 % tm == 0 and n % tn == 0
    r0 = row0 // tm
    return pl.pallas_call(
        functools.partial(_mm_kernel, act=act),
        grid=(rows // tm, n // tn),
        in_specs=[pl.BlockSpec((tm, k), lambda i, j: (i + r0, 0)),
                  pl.BlockSpec((k, tn), lambda i, j: (0, j))],
        out_specs=pl.BlockSpec((tm, tn), lambda i, j: (i, j)),
        out_shape=jax.ShapeDtypeStruct((rows, n), out_dtype),
        compiler_params=_params(("arbitrary", "arbitrary")),
        name=name,
    )(a, w)


def _stage_weights(i, pairs):
    @pl.when(i == 0)
    def _():
        for w_ref, w_scr in pairs:
            w_scr[...] = w_ref[0].astype(BF16)


def _merge_kernel(oa_c, ob_c, oc_c, oa_d, ob_d, oc_d, ga_ref, gb_ref, gc_ref, wa_ref, wb_ref, wc_ref, o_ref,
                  wa_scr, wb_scr, wc_scr, *, n_a):
    i = pl.program_id(1)
    _stage_weights(i, ((wa_ref, wa_scr), (wb_ref, wb_scr), (wc_ref, wc_scr)))

    def body(oa_ref, ob_ref, oc_ref):
        y = ga_ref[...].astype(F32) * _dot(oa_ref[...], wa_scr[...])
        y += gb_ref[...].astype(F32) * _dot(ob_ref[...], wb_scr[...])
        y += gc_ref[...].astype(F32) * _dot(oc_ref[...], wc_scr[...])
        o_ref[...] = y.astype(o_ref.dtype)

    @pl.when(i < n_a)
    def _():
        body(oa_c, ob_c, oc_c)

    @pl.when(i >= n_a)
    def _():
        body(oa_d, ob_d, oc_d)


def branch_merge(o_ctx, o_dec, gates, wa, wb, wc, layer, tm=512, tn=1024):
    m = M_ALL
    nb = D_MODEL // tn
    n_a = M_CTX // tm
    amap, bmap = _two_source_maps(n_a, False)
    col = lambda j, i: (layer, 0, j)
    widths = (GLA_V_W, DIFF_V_W, NA_W)
    return pl.pallas_call(
        functools.partial(_merge_kernel, n_a=n_a),
        grid=(nb, m // tm),
        in_specs=[pl.BlockSpec((tm, w), lambda j, i: (amap(i), 0)) for w in widths]
                 + [pl.BlockSpec((tm, w), lambda j, i: (bmap(i), 0)) for w in widths]
                 + [pl.BlockSpec((tm, tn), lambda j, i: (i, j)),
                  pl.BlockSpec((tm, tn), lambda j, i: (i, j + nb)),
                  pl.BlockSpec((tm, tn), lambda j, i: (i, j + 2 * nb)),
                  pl.BlockSpec((1, GLA_V_W, tn), col),
                  pl.BlockSpec((1, DIFF_V_W, tn), col),
                  pl.BlockSpec((1, NA_W, tn), col)],
        out_specs=pl.BlockSpec((tm, tn), lambda j, i: (i, j)),
        out_shape=jax.ShapeDtypeStruct((m, D_MODEL), BF16),
        scratch_shapes=[pltpu.VMEM((GLA_V_W, tn), BF16), pltpu.VMEM((DIFF_V_W, tn), BF16),
                        pltpu.VMEM((NA_W, tn), BF16)],
        compiler_params=_params(("arbitrary", "arbitrary")),
        name="branch_merge",
    )(*o_ctx, *o_dec, gates, gates, gates, wa, wb, wc)


def _two_source_maps(n_a, stacked):
    b_off = n_a if stacked else 0
    return (lambda i: jnp.minimum(i, n_a - 1)), (lambda i: jnp.maximum(i - n_a, 0) + b_off)


def _resid_mm_kernel(a_ref, w_ref, xa_ref, xb_ref, gt_ref, o_ref, w_scr, *, n_a):
    i = pl.program_id(1)
    _stage_weights(i, ((w_ref, w_scr),))
    y = gt_ref[0] * _dot(a_ref[...], w_scr[...])

    @pl.when(i < n_a)
    def _():
        o_ref[...] = xa_ref[...] + y

    @pl.when(i >= n_a)
    def _():
        o_ref[...] = xb_ref[...] + y


def resid_matmul(a, w, xs, gt, layer, tm, tn, name):
    m, k = a.shape
    n = w.shape[2]
    n_a = M_CTX // tm
    amap, bmap = _two_source_maps(n_a, len(xs) == 1)
    xa, xb = (xs[0], xs[0]) if len(xs) == 1 else xs
    return pl.pallas_call(
        functools.partial(_resid_mm_kernel, n_a=n_a),
        grid=(n // tn, m // tm),
        in_specs=[pl.BlockSpec((tm, k), lambda j, i: (i, 0)),
                  pl.BlockSpec((1, k, tn), lambda j, i: (layer, 0, j)),
                  pl.BlockSpec((tm, tn), lambda j, i: (amap(i), j)),
                  pl.BlockSpec((tm, tn), lambda j, i: (bmap(i), j)),
                  pl.BlockSpec((1, 1, tn), lambda j, i: (_cond_of_row(i * tm), 0, j))],
        out_specs=pl.BlockSpec((tm, tn), lambda j, i: (i, j)),
        out_shape=jax.ShapeDtypeStruct((m, n), F32),
        scratch_shapes=[pltpu.VMEM((k, tn), BF16)],
        compiler_params=_params(("arbitrary", "arbitrary")),
        name=name,
    )(a, w, xa, xb, gt)


def _ffn_up_kernel(h_ref, wg_ref, wu_ref, o_ref, wg_scr, wu_scr):
    _stage_weights(pl.program_id(1), ((wg_ref, wg_scr), (wu_ref, wu_scr)))
    h = h_ref[...]
    g = _dot(h, wg_scr[...])
    u = _dot(h, wu_scr[...])
    o_ref[...] = (g * jax.nn.sigmoid(g) * u).astype(o_ref.dtype)


def ffn_up(h, wg, wu, layer, tm=2048, tn=512):
    m = h.shape[0]
    wspec = pl.BlockSpec((1, D_MODEL, tn), lambda j, i: (layer, 0, j))
    return pl.pallas_call(
        _ffn_up_kernel,
        grid=(D_FF // tn, m // tm),
        in_specs=[pl.BlockSpec((tm, D_MODEL), lambda j, i: (i, 0)), wspec, wspec],
        out_specs=pl.BlockSpec((tm, tn), lambda j, i: (i, j)),
        out_shape=jax.ShapeDtypeStruct((m, D_FF), BF16),
        scratch_shapes=[pltpu.VMEM((D_MODEL, tn), BF16), pltpu.VMEM((D_MODEL, tn), BF16)],
        compiler_params=_params(("arbitrary", "arbitrary")),
        name="ffn_up",
    )(h, wg, wu)


def _gla_kernel(q_ref, k_ref, v_ref, gr_ref, lr_ref, wd_ref, bd_ref, g_ref, *rest, nblk, has_state):
    if has_state:
        s0f_ref, s0b_ref, o_ref, sfin_ref, s_scr, of_scr = rest
    else:
        o_ref, sfin_ref, s_scr, of_scr = rest
    pss = pl.program_id(1)
    blk = pl.program_id(2)
    C = GLA_CHUNK
    nch = GLA_BLOCK // C

    if has_state:
        @pl.when(jnp.logical_and(blk == 0, pss == 0))
        def _init_f():
            s_scr[...] = s0f_ref[0, 0]

        @pl.when(jnp.logical_and(blk == 0, pss == 1))
        def _init_b():
            s_scr[...] = s0b_ref[0, 0]
    else:
        @pl.when(blk == 0)
        def _init():
            s_scr[...] = jnp.zeros_like(s_scr)

    lr = lr_ref[...]
    lr_hi = lr.astype(BF16)
    lr_lo = (lr - lr_hi.astype(F32)).astype(BF16)
    wd = wd_ref[0]
    wd_hi = wd.astype(BF16)
    wd_lo = (wd - wd_hi.astype(F32)).astype(BF16)
    pre = _dot(lr_hi, wd_hi) + _dot(lr_hi, wd_lo) + _dot(lr_lo, wd_hi) + bd_ref[0]
    la_all = jax.nn.log_sigmoid(pre) * (1.0 / GLA_TAU)

    row = lax.broadcasted_iota(jnp.int32, (C, C), 0)
    colm = lax.broadcasted_iota(jnp.int32, (C, C), 1)
    lane = lax.broadcasted_iota(jnp.int32, (1, LANES), 1)
    head_mask = (lane < GLA_DK, lane >= GLA_DK)
    eye = (lax.broadcasted_iota(jnp.int32, (LANES, LANES), 0)
           == lax.broadcasted_iota(jnp.int32, (LANES, LANES), 1))

    def run_direction(backward):
        keep = (colm >= row) if backward else (colm <= row)
        tri = jnp.where(keep, 1.0, 0.0).astype(BF16)
        last, mid = (0, C // 2) if backward else (C - 1, C // 2 - 1)
        order = range(nch - 1, -1, -1) if backward else range(nch)
        for c in order:
            rs = slice(c * C, (c + 1) * C)
            for p in range(GLA_HEADS // 2):
                ls = slice(p * LANES, (p + 1) * LANES)
                la = la_all[rs, ls]
                hi, md, lo = _split3(la)
                b = _dot(tri, hi) + _dot(tri, md) + _dot(tri, lo)
                b_last = b[last:last + 1, :]
                b_mid = b[mid:mid + 1, :]
                q = q_ref[rs, ls] * (GLA_DK ** -0.5)
                k = k_ref[rs, ls]
                q_in = q * jnp.exp(b - b_mid)
                k_in = (k * jnp.exp(b_mid - b)).astype(BF16)
                q_ex = q * jnp.exp(b)
                k_dc = k * jnp.exp(b_last - b)
                dec_col = jnp.exp(jnp.sum(jnp.where(eye, b_last, 0.0), axis=1, keepdims=True))
                s_old = s_scr[p]
                s_old_bf = s_old.astype(BF16)
                s_new = dec_col * s_old
                for h in range(2):
                    hd = 2 * p + h
                    vs = slice(hd * GLA_DV, (hd + 1) * GLA_DV)
                    v = v_ref[rs, vs].astype(BF16)
                    att = _dot_nt(jnp.where(head_mask[h], q_in, 0.0).astype(BF16), k_in)
                    att = jnp.where(keep, att, 0.0).astype(BF16)
                    o = _dot(att, v) + _dot(jnp.where(head_mask[h], q_ex, 0.0).astype(BF16), s_old_bf)
                    s_new = s_new + _dot_tn(jnp.where(head_mask[h], k_dc, 0.0).astype(BF16), v)
                    if backward:
                        tot = of_scr[blk_rows(c), vs] + o
                        y = tot * lax.rsqrt(jnp.mean(tot * tot, axis=-1, keepdims=True) + NORM_EPS) * g_ref[0]
                        gr = gr_ref[rs, vs]
                        o_ref[rs, vs] = (y * (gr * jax.nn.sigmoid(gr))).astype(o_ref.dtype)
                    else:
                        of_scr[blk_rows(c), vs] = o
                s_scr[p] = s_new

    def blk_rows(c):
        pos = jnp.where(pss == 0, blk, nblk - 1 - blk)
        return pl.ds(pl.multiple_of(pos * GLA_BLOCK + c * C, C), C)

    @pl.when(pss == 0)
    def _fwd():
        run_direction(False)

    @pl.when(pss == 1)
    def _bwd():
        run_direction(True)

    @pl.when(blk == nblk - 1)
    def _fin():
        sfin_ref[0, 0] = s_scr[...]


def gla_mixer(pa, wd, bd, g, layer, *, row0, nseq, seqlen, state=None):
    nblk = seqlen // GLA_BLOCK
    b0 = row0 // GLA_BLOCK
    has_state = state is not None

    def rb(s, p, b):
        return b0 + s * nblk + b + p * (nblk - 1 - 2 * b)

    def orb(s, p, b):
        return s * nblk + nblk - 1 - p * b

    in_specs = [pl.BlockSpec((GLA_BLOCK, GLA_QK_W), lambda s, p, b: (rb(s, p, b), 0)),
                pl.BlockSpec((GLA_BLOCK, GLA_QK_W), lambda s, p, b: (rb(s, p, b), 1)),
                pl.BlockSpec((GLA_BLOCK, GLA_V_W), lambda s, p, b: (rb(s, p, b), 1)),
                pl.BlockSpec((GLA_BLOCK, GLA_V_W), lambda s, p, b: (rb(s, p, b), 2)),
                pl.BlockSpec((GLA_BLOCK, LANES), lambda s, p, b: (rb(s, p, b), GLA_A_W // LANES - 1)),
                pl.BlockSpec((1, LANES, GLA_QK_W), lambda s, p, b: (p, 0, 0)),
                pl.BlockSpec((1, 1, GLA_QK_W), lambda s, p, b: (p, 0, 0)),
                pl.BlockSpec((1, 1, GLA_DV), lambda s, p, b: (layer, 0, 0))]
    args = [pa, pa, pa, pa, pa, wd, bd, g]
    if has_state:
        st_spec = pl.BlockSpec((1, 1, 2, LANES, LANES), lambda s, p, b: (s, layer, 0, 0, 0))
        in_specs += [st_spec, st_spec]
        args += [state[0], state[1]]
    kern = functools.partial(_gla_kernel, nblk=nblk, has_state=has_state)
    return pl.pallas_call(
        kern,
        grid=(nseq, 2, nblk),
        in_specs=in_specs,
        out_specs=[pl.BlockSpec((GLA_BLOCK, GLA_V_W), lambda s, p, b: (orb(s, p, b), 0)),
                   pl.BlockSpec((1, 1, 2, LANES, LANES), lambda s, p, b: (p, s, 0, 0, 0))],
        out_shape=[jax.ShapeDtypeStruct((nseq * seqlen, GLA_V_W), BF16),
                   jax.ShapeDtypeStruct((2, nseq, 2, LANES, LANES), F32)],
        scratch_shapes=[pltpu.VMEM((2, LANES, LANES), F32),
                        pltpu.VMEM((seqlen, GLA_V_W), F32)],
        compiler_params=_params(("arbitrary", "arbitrary", "arbitrary")),
        name="gla_mixer",
    )(*args)


def _rope_kernel(x_ref, cos_ref, sin_ref, o_ref):
    cos = cos_ref[...]
    sin = sin_ref[...]
    lane = lax.broadcasted_iota(jnp.int32, (1, LANES), 1)
    first = (lane % (2 * ROPE_NFREQ)) < ROPE_NFREQ
    nq = DIFF_QK_W // LANES
    for j in range(2 * nq):
        x = x_ref[:, j * LANES:(j + 1) * LANES].astype(F32)
        partner = jnp.where(first, pltpu.roll(x, LANES - ROPE_NFREQ, 1), pltpu.roll(x, ROPE_NFREQ, 1))
        y = x * cos + partner * sin
        if j < nq:
            y = y * DIFF_QSCALE
        o_ref[:, j * LANES:(j + 1) * LANES] = y.astype(o_ref.dtype)


def rope_qk(pb_dec, cos_t, sin_t, tm=512):
    nt = DEC_SEQ // tm
    return pl.pallas_call(
        _rope_kernel,
        grid=(M_DEC // tm,),
        in_specs=[pl.BlockSpec((tm, 2 * DIFF_QK_W), lambda i: (i, 0)),
                  pl.BlockSpec((tm, LANES), lambda i: (i % nt, 0)),
                  pl.BlockSpec((tm, LANES), lambda i: (i % nt, 0))],
        out_specs=pl.BlockSpec((tm, 2 * DIFF_QK_W), lambda i: (i, 0)),
        out_shape=jax.ShapeDtypeStruct((M_DEC, 2 * DIFF_QK_W), BF16),
        compiler_params=_params(("arbitrary",)),
        name="rope_qk",
    )(pb_dec, cos_t, sin_t)


def _diff_head(q, chunks, lam, g, lam_init):
    lane = lax.broadcasted_iota(jnp.int32, (1, LANES), 1)
    qm = [jnp.where(lane < DIFF_DK, q, jnp.zeros_like(q)), jnp.where(lane >= DIFF_DK, q, jnp.zeros_like(q))]
    m = [None, None]
    acc = [None, None]
    for k, v in chunks:
        for mp in range(2):
            s = _dot_nt(qm[mp], k)
            cm = jnp.max(s, axis=-1, keepdims=True)
            if m[mp] is None:
                m[mp] = cm
                acc[mp] = _dot(jnp.exp2(s - cm).astype(BF16), v)
            else:
                m_new = jnp.maximum(m[mp], cm)
                acc[mp] = jnp.exp2(m[mp] - m_new) * acc[mp] + _dot(jnp.exp2(s - m_new).astype(BF16), v)
                m[mp] = m_new
    outs = [a[:, :LANES] / a[:, LANES:] for a in acc]
    o = outs[0] - lam * outs[1]
    y = o * lax.rsqrt(jnp.mean(o * o, axis=-1, keepdims=True) + NORM_EPS) * g
    return y * (1.0 - lam_init)


def _diff_kernel(lam_ref, q_ref, kn_ref, vn_ref, kc_ref, vc_ref, g_ref, o_ref, vaug, kc_scr, *, lam_init):
    n_new = kn_ref.shape[0]
    n_c = PAST_LEN

    @pl.when(pl.program_id(2) == 0)
    def _stage():
        vaug[:, LANES:] = jnp.ones((n_c + n_new, LANES), BF16)
        vaug[n_c:, :LANES] = vn_ref[...]
        vaug[:n_c, :LANES] = vc_ref[0, 0].astype(BF16)
        kc_scr[...] = kc_ref[0, 0].astype(BF16)

    chunks = [(kc_scr[...], vaug[:n_c, :])]
    for c in range(n_new // DIFF_KCHUNK):
        rows = slice(c * DIFF_KCHUNK, (c + 1) * DIFF_KCHUNK)
        chunks.append((kn_ref[rows, :], vaug[n_c + c * DIFF_KCHUNK:n_c + (c + 1) * DIFF_KCHUNK, :]))
    y = _diff_head(q_ref[...], chunks, lam_ref[0], g_ref[0], lam_init)
    o_ref[...] = y.astype(o_ref.dtype)


def _diff_ctx_kernel(lam_ref, q_ref, k_ref, v_ref, g_ref, o_ref, *, lam_init):
    ones = jnp.ones((SEQ, LANES), BF16)
    for h in range(DIFF_HEADS):
        ls = slice(h * LANES, (h + 1) * LANES)
        q = (q_ref[:, ls] * DIFF_QSCALE).astype(BF16)
        vaug = jnp.concatenate([v_ref[:, ls].astype(BF16), ones], axis=1)
        y = _diff_head(q, [(k_ref[:, ls].astype(BF16), vaug)], lam_ref[0], g_ref[0], lam_init)
        o_ref[:, ls] = y.astype(o_ref.dtype)


def diff_attention_ctx(pb_ctx, lam, g, layer, lam_init):
    return pl.pallas_call(
        functools.partial(_diff_ctx_kernel, lam_init=lam_init),
        grid=(BATCH,),
        in_specs=[pl.BlockSpec(memory_space=pltpu.SMEM),
                  pl.BlockSpec((SEQ, DIFF_QK_W), lambda s: (s, 0)),
                  pl.BlockSpec((SEQ, DIFF_QK_W), lambda s: (s, 1)),
                  pl.BlockSpec((SEQ, DIFF_V_W), lambda s: (s, 2)),
                  pl.BlockSpec((1, 1, DIFF_DV), lambda s: (layer, 0, 0))],
        out_specs=pl.BlockSpec((SEQ, DIFF_V_W), lambda s: (s, 0)),
        out_shape=jax.ShapeDtypeStruct((M_CTX, DIFF_V_W), BF16),
        compiler_params=_params(("arbitrary",)),
        name="diff_attention_ctx",
    )(lam, pb_ctx, pb_ctx, pb_ctx, g)


def diff_attention_dec(qk_rope, pb_dec, cache_k, cache_v, layer, lam, g, lam_init, tq=1024):
    kern = functools.partial(_diff_kernel, lam_init=lam_init)
    H = DIFF_HEADS
    nqb = DEC_SEQ // tq
    ntot = PAST_LEN + DEC_SEQ
    return pl.pallas_call(
        kern,
        grid=(DEC_BATCH, H, nqb),
        in_specs=[pl.BlockSpec(memory_space=pltpu.SMEM),
                  pl.BlockSpec((tq, LANES), lambda s, h, b: (s * nqb + b, h)),
                  pl.BlockSpec((DEC_SEQ, LANES), lambda s, h, b: (s, H + h)),
                  pl.BlockSpec((DEC_SEQ, LANES), lambda s, h, b: (s, 2 * H + h)),
                  pl.BlockSpec((1, 1, PAST_LEN, LANES), lambda s, h, b: (s, layer, 0, h)),
                  pl.BlockSpec((1, 1, PAST_LEN, LANES), lambda s, h, b: (s, layer, 0, h)),
                  pl.BlockSpec((1, 1, DIFF_DV), lambda s, h, b: (layer, 0, 0))],
        out_specs=pl.BlockSpec((tq, LANES), lambda s, h, b: (s * nqb + b, h)),
        out_shape=jax.ShapeDtypeStruct((M_DEC, DIFF_V_W), BF16),
        scratch_shapes=[pltpu.VMEM((ntot, 2 * LANES), BF16),
                        pltpu.VMEM((PAST_LEN, LANES), BF16)],
        compiler_params=_params(("arbitrary", "arbitrary", "arbitrary")),
        name="diff_attention_dec",
    )(lam, qk_rope, qk_rope, pb_dec, cache_k, cache_v, g)


def _softmax_pair(q, keys_vals, biases):
    lane = lax.broadcasted_iota(jnp.int32, (1, LANES), 1)
    outs = []
    for h in range(2):
        msk = (lane < NA_DK) if h == 0 else (lane >= NA_DK)
        qm = jnp.where(msk, q, jnp.zeros_like(q))
        ss = []
        for (k, _), bias in zip(keys_vals, biases):
            s = _dot_nt(qm, k)
            if bias is not None:
                s = s + bias(h)
            ss.append(s)
        m = functools.reduce(jnp.maximum, [jnp.max(s, axis=-1, keepdims=True) for s in ss])
        num = 0.0
        den = 0.0
        for s, (_, v) in zip(ss, keys_vals):
            e = jnp.exp(s - m)
            den = den + jnp.sum(e, axis=-1, keepdims=True)
            num = num + _dot(e.astype(BF16), v)
        outs.append(num / den)
    return jnp.where(lane < NA_DK, outs[0], outs[1])


def _ctx_attn_kernel(q_ref, k_ref, v_ref, o_ref):
    for p in range(NA_HEADS // 2):
        ls = slice(p * LANES, (p + 1) * LANES)
        q = (q_ref[:, ls] * (NA_DK ** -0.5)).astype(BF16)
        k = k_ref[:, ls].astype(BF16)
        v = v_ref[:, ls].astype(BF16)
        o_ref[:, ls] = _softmax_pair(q, [(k, v)], [None]).astype(o_ref.dtype)


def context_attention_ctx(pc_ctx):
    return pl.pallas_call(
        _ctx_attn_kernel,
        grid=(BATCH,),
        in_specs=[pl.BlockSpec((SEQ, NA_W), lambda s: (s, 0)),
                  pl.BlockSpec((SEQ, NA_W), lambda s: (s, 1)),
                  pl.BlockSpec((SEQ, NA_W), lambda s: (s, 2))],
        out_specs=pl.BlockSpec((SEQ, NA_W), lambda s: (s, 0)),
        out_shape=jax.ShapeDtypeStruct((M_CTX, NA_W), BF16),
        compiler_params=_params(("arbitrary",)),
        name="context_attention",
    )(pc_ctx, pc_ctx, pc_ctx)


def _na_window_start(rb):
    return jnp.clip(NA_QROWS * rb - NA_WIN_H // 2, 0, GRID_H - NA_KROWS)


def _na_kernel(q_ref, k_ref, v_ref, kc_ref, vc_ref, bias_ref, o_ref, kc_scr, vc_scr):
    rb = pl.program_id(1)

    @pl.when(rb == 0)
    def _stage():
        kc_scr[...] = kc_ref[0, 0].astype(BF16)
        vc_scr[...] = vc_ref[0, 0].astype(BF16)

    start = pl.multiple_of(_na_window_start(rb) * GRID_W, GRID_W)
    for p in range(NA_HEADS // 2):
        ls = slice(p * LANES, (p + 1) * LANES)
        kl = k_ref[pl.ds(start, NA_NLOC), ls]
        vl = v_ref[pl.ds(start, NA_NLOC), ls]
        q = q_ref[:, ls] * (NA_DK ** -0.5)
        o = _softmax_pair(q, [(kl, vl), (kc_scr[:, ls], vc_scr[:, ls])],
                          [lambda h, p=p: bias_ref[0, 2 * p + h], None])
        o_ref[:, ls] = o.astype(o_ref.dtype)


def _na_situation(rb):
    nrb = GRID_H // NA_QROWS
    return jnp.where(rb < 2, rb, jnp.where(rb >= nrb - 2, rb - (nrb - 5), 2))


def neighbourhood_attention_dec(pc_dec, cache_k, cache_v, layer, bias):
    nrb = GRID_H // NA_QROWS
    cspec = pl.BlockSpec((1, 1, PAST_LEN, NA_W), lambda s, r: (s, layer, 0, 0))
    return pl.pallas_call(
        _na_kernel,
        grid=(DEC_BATCH, nrb),
        in_specs=[pl.BlockSpec((NA_QBLK, NA_W), lambda s, r: (s * nrb + r, 0)),
                  pl.BlockSpec((DEC_SEQ, NA_W), lambda s, r: (s, 1)),
                  pl.BlockSpec((DEC_SEQ, NA_W), lambda s, r: (s, 2)),
                  cspec, cspec,
                  pl.BlockSpec((1, NA_HEADS, NA_QBLK, NA_NLOC), lambda s, r: (_na_situation(r), 0, 0, 0))],
        out_specs=pl.BlockSpec((NA_QBLK, NA_W), lambda s, r: (s * nrb + r, 0)),
        out_shape=jax.ShapeDtypeStruct((M_DEC, NA_W), BF16),
        scratch_shapes=[pltpu.VMEM((PAST_LEN, NA_W), BF16), pltpu.VMEM((PAST_LEN, NA_W), BF16)],
        compiler_params=_params(("arbitrary", "arbitrary")),
        name="neighbourhood_attention",
    )(pc_dec, pc_dec, pc_dec, cache_k, cache_v, bias)


def _na_bias_tables(rpb):
    nrb = GRID_H // NA_QROWS
    rbs = [0, 1, 2, nrb - 2, nrb - 1]
    qc = np.arange(GRID_W)
    kc = np.arange(GRID_W)
    col_start = np.clip(qc - NA_WIN_W // 2, 0, GRID_W - NA_WIN_W)
    col_ok = (kc[None, :] >= col_start[:, None]) & (kc[None, :] < col_start[:, None] + NA_WIN_W)
    dcol = np.clip(kc[None, :] - qc[:, None], 1 - NA_WIN_W, NA_WIN_W - 1) + NA_WIN_W - 1
    drow_all, ok_all = [], []
    for rb in rbs:
        ws = int(np.clip(NA_QROWS * rb - NA_WIN_H // 2, 0, GRID_H - NA_KROWS))
        qr = NA_QROWS * rb + np.arange(NA_QROWS)
        kr = ws + np.arange(NA_KROWS)
        win = np.clip(qr - NA_WIN_H // 2, 0, GRID_H - NA_WIN_H)
        row_ok = (kr[None, :] >= win[:, None]) & (kr[None, :] < win[:, None] + NA_WIN_H)
        drow_all.append(np.clip(kr[None, :] - qr[:, None] + NA_WIN_H - 1, 0, 2 * NA_WIN_H - 2))
        ok_all.append(row_ok[:, None, :, None] & col_ok[None, :, None, :])
    nd, nc = 2 * NA_WIN_H - 1, 2 * NA_WIN_W - 1
    sel_row = (np.stack(drow_all).reshape(-1)[:, None] == np.arange(nd)[None, :]).astype(np.float32)
    sel_col = (np.arange(nc)[:, None] == dcol.reshape(-1)[None, :]).astype(np.float32)
    tab = jnp.einsum("nd,hdc,cx->hnx", sel_row, rpb.astype(F32), sel_col, precision=lax.Precision.HIGHEST)
    tab = tab.reshape(NA_HEADS, 5, NA_QROWS, NA_KROWS, GRID_W, GRID_W)
    tab = jnp.transpose(tab, (1, 0, 2, 4, 3, 5))
    ok = np.stack(ok_all)[:, None]
    return jnp.where(ok, tab, NEG_INF).reshape(5, NA_HEADS, NA_QBLK, NA_NLOC)


def _rope_tables():
    pos = np.arange(DEC_SEQ)
    row = (pos // GRID_W).astype(np.float32)
    col = (pos % GRID_W).astype(np.float32)
    inv = jnp.asarray(ROPE_BASE, F32) ** (-jnp.arange(ROPE_NFREQ, dtype=F32) / ROPE_NFREQ)
    ang_r = jnp.asarray(row)[:, None] * inv
    ang_c = jnp.asarray(col)[:, None] * inv
    cos64 = jnp.concatenate([jnp.cos(ang_r), jnp.cos(ang_r), jnp.cos(ang_c), jnp.cos(ang_c)], axis=1)
    sin64 = jnp.concatenate([-jnp.sin(ang_r), jnp.sin(ang_r), -jnp.sin(ang_c), jnp.sin(ang_c)], axis=1)
    return jnp.tile(cos64, (1, 2)), jnp.tile(sin64, (1, 2))


def kernel(x_prompt, x_sample, c, cache_diff_k, cache_diff_v, cache_na_k, cache_na_v, state_gla_fwd,
           state_gla_bwd, c_ctx, w_ada, b_ada, norm_mix_g, w_in, w_decay, b_decay, gla_norm_g, diff_lambda,
           diff_norm_g, na_rpb, w_br_a, w_br_b, w_br_c, w_out, norm_ffn_g, w_ffn_gate, w_ffn_up, w_ffn_down,
           norm_final_g):
    D = D_MODEL
    xs = (x_prompt.reshape(M_CTX, D), x_sample.reshape(M_DEC, D))
    cond8 = jnp.concatenate([c_ctx[None], c, jnp.zeros((8 - N_COND, D), F32)], axis=0)
    mod = modulation_all(cond8, w_ada, b_ada)
    cos_t, sin_t = _rope_tables()
    g_mix = norm_mix_g.reshape(DEPTH, 1, D)
    g_ffn = norm_ffn_g.reshape(DEPTH, 1, D)
    g_gla = gla_norm_g.reshape(DEPTH, 1, GLA_DV)
    g_diff = diff_norm_g.reshape(DEPTH, 1, DIFF_DV)
    ck_diff = cache_diff_k.reshape(DEC_BATCH, DEPTH, PAST_LEN, DIFF_QK_W)
    cv_diff = cache_diff_v.reshape(DEC_BATCH, DEPTH, PAST_LEN, DIFF_V_W)
    ck_na = cache_na_k.reshape(DEC_BATCH, DEPTH, PAST_LEN, NA_W)
    cv_na = cache_na_v.reshape(DEC_BATCH, DEPTH, PAST_LEN, NA_W)
    gla_state = (state_gla_fwd.reshape(DEC_BATCH, DEPTH, 2, LANES, LANES),
                 state_gla_bwd.reshape(DEC_BATCH, DEPTH, 2, LANES, LANES))

    offs = np.cumsum((0,) + IN_WIDTHS)
    a_end, b_end, c_end = offs[5], offs[8], offs[11]
    new_state = []
    for l in range(DEPTH):
        lam_init = 0.8 - 0.6 * float(np.exp(-0.3 * l))
        m6 = mod[l, :N_COND].reshape(N_COND, 6, 1, D)
        sh_a, sc_a, gt_a, sh_f, sc_f, gt_f = (m6[:, i] for i in range(6))
        wl = w_in[l]
        w_a = jnp.concatenate([wl[:, :a_end], jnp.zeros((D, GLA_A_W - a_end), F32)], axis=1).astype(BF16)
        w_b = wl[:, a_end:b_end].astype(BF16)
        w_c = wl[:, b_end:c_end].astype(BF16)
        w_g = wl[:, c_end:].astype(BF16)

        h = norm_mod(xs, g_mix, l, sc_a, sh_a)
        pa = matmul(h, w_a, F32, tm=512, tn=GLA_A_W, name="in_proj_gla")
        pb_ctx = matmul(h, w_b, F32, rows=M_CTX, name="in_proj_diff_ctx")
        pb_dec = matmul(h, w_b, BF16, row0=M_CTX, rows=M_DEC, name="in_proj_diff_dec")
        pc_ctx = matmul(h, w_c, F32, rows=M_CTX, tn=768, name="in_proj_na_ctx")
        pc_dec = matmul(h, w_c, BF16, row0=M_CTX, rows=M_DEC, tn=768, name="in_proj_na_dec")
        gates = matmul(h, w_g, BF16, tm=2048, act="sigmoid", name="in_proj_gates")

        wd = jnp.zeros((2, LANES, GLA_QK_W), F32)
        wd = wd.at[0, :GLA_LR].set(w_decay[l, 0]).at[1, GLA_LR:2 * GLA_LR].set(w_decay[l, 1])
        bd = b_decay[l][:, None, :]
        oa_c, s_fin = gla_mixer(pa, wd, bd, g_gla, l, row0=0, nseq=BATCH, seqlen=SEQ)
        oa_d, _ = gla_mixer(pa, wd, bd, g_gla, l, row0=M_CTX, nseq=DEC_BATCH, seqlen=DEC_SEQ, state=gla_state)

        lp = diff_lambda[l].astype(F32)
        lam = (jnp.exp(jnp.sum(lp[0] * lp[1])) - jnp.exp(jnp.sum(lp[2] * lp[3])) + lam_init).reshape(1)
        ob_c = diff_attention_ctx(pb_ctx, lam, g_diff, l, lam_init)
        qk_rope = rope_qk(pb_dec, cos_t, sin_t)
        ob_d = diff_attention_dec(qk_rope, pb_dec, ck_diff, cv_diff, l, lam, g_diff, lam_init)

        oc_c = context_attention_ctx(pc_ctx)
        oc_d = neighbourhood_attention_dec(pc_dec, ck_na, cv_na, l, _na_bias_tables(na_rpb[l]))

        merged = branch_merge((oa_c, ob_c, oc_c), (oa_d, ob_d, oc_d), gates, w_br_a, w_br_b, w_br_c, l)
        x = resid_matmul(merged, w_out, xs, gt_a, l, tm=512, tn=1024, name="out_proj")
        xs = (x,)

        h2 = norm_mod(xs, g_ffn, l, sc_f, sh_f)
        act = ffn_up(h2, w_ffn_gate, w_ffn_up, l)
        x = resid_matmul(act, w_ffn_down, xs, gt_f, l, tm=512, tn=512, name="ffn_down")
        xs = (x,)

        new_state.append((
            pb_ctx[:, DIFF_QK_W:2 * DIFF_QK_W].reshape(BATCH, SEQ, DIFF_HEADS, 2 * DIFF_DK),
            pb_ctx[:, 2 * DIFF_QK_W:].reshape(BATCH, SEQ, DIFF_HEADS, DIFF_DV),
            pc_ctx[:, NA_W:2 * NA_W].reshape(BATCH, SEQ, NA_HEADS, NA_DK),
            pc_ctx[:, 2 * NA_W:].reshape(BATCH, SEQ, NA_HEADS, NA_DK),
            s_fin[0].reshape(BATCH, GLA_HEADS, GLA_DK, GLA_DV),
            s_fin[1].reshape(BATCH, GLA_HEADS, GLA_DK, GLA_DV)))

    gfin = norm_final_g[None]
    y_prompt = final_norm(x, gfin, 0, M_CTX).reshape(BATCH, SEQ, D)
    y_sample = final_norm(x, gfin, M_CTX, M_DEC).reshape(DEC_BATCH, DEC_SEQ, D)
    stacked = tuple(jnp.stack([st[i] for st in new_state], axis=1) for i in range(6))
    return (y_prompt, y_sample) + stacked
```
